```python
import math
import jax, jax.numpy as jnp
from jax import lax
import numpy as np

D_MODEL = 1024
BATCH = 1
SEQ = 16384
DEPTH = 1
DEC_BATCH = 128
DEC_SEQ = 4
PAST_LEN = 8192
PAGE_SIZE = 128

HEAD_DIM = 64
N_HEADS = D_MODEL // HEAD_DIM
N_HEADS_A = N_HEADS // 2
N_HEADS_B = N_HEADS - N_HEADS_A
N_KV_A = N_HEADS_A // 2
N_KV_B = N_HEADS_B // 2
GROUP_A = N_HEADS_A // N_KV_A
GROUP_B = N_HEADS_B // N_KV_B
MOBA_BLOCK = 256
MOBA_TOPK = 3
IDX_HEADS = 8
IDX_DIM = 64
DSA_TOPK = 256
D_FF = -(-8 * D_MODEL // (3 * 256)) * 256
N_BUCKETS = 32
MAX_DISTANCE = 128
Q_BLOCK = 128
EPS = 1e-6
POOL_NUM = 5
POOL_DEN = 4
IN_WIDTHS = (N_HEADS_A * HEAD_DIM, N_KV_A * HEAD_DIM, N_KV_A * HEAD_DIM,
             N_HEADS_B * HEAD_DIM, N_KV_B * HEAD_DIM, N_KV_B * HEAD_DIM,
             IDX_HEADS * IDX_DIM, IDX_DIM, IDX_HEADS)

kernel_name = "hymba_moba_dsa_adaln_step"

F32 = jnp.float32


def rms_norm(x, g):
    xf = x.astype(F32)
    y = xf * lax.rsqrt(jnp.mean(xf * xf, axis=-1, keepdims=True) + EPS)
    return (y * g).astype(x.dtype)


def t5_bucket(dist):
    n = jnp.maximum(dist, 0)
    exact = N_BUCKETS // 2
    log_ratio = jnp.log(jnp.maximum(n, 1).astype(F32) / exact) / math.log(MAX_DISTANCE / exact)
    large = jnp.minimum(exact + (log_ratio * (N_BUCKETS - exact)).astype(jnp.int32), N_BUCKETS - 1)
    return jnp.where(n < exact, n, large)


def modulation(c, w_ada, b_ada):
    m = (jax.nn.silu(c) @ w_ada + b_ada)[:, None, :]
    return jnp.split(m, 6, axis=-1)


def modulate(xn, shift, scale):
    return xn * (1 + scale) + shift


def project_in(h, w_in, g_qa, g_ka, g_qb, g_kb):
    n, t, _ = h.shape
    splits = np.cumsum(IN_WIDTHS)[:-1].tolist()
    qa, ka, va, qb, kb, vb, qi, ki, wi = jnp.split(h @ w_in, splits, axis=-1)
    qa = rms_norm(qa.reshape(n, t, N_HEADS_A, HEAD_DIM), g_qa)
    ka = rms_norm(ka.reshape(n, t, N_KV_A, HEAD_DIM), g_ka)
    va = va.reshape(n, t, N_KV_A, HEAD_DIM)
    qb = rms_norm(qb.reshape(n, t, N_HEADS_B, HEAD_DIM), g_qb)
    kb = rms_norm(kb.reshape(n, t, N_KV_B, HEAD_DIM), g_kb)
    vb = vb.reshape(n, t, N_KV_B, HEAD_DIM)
    qi = qi.reshape(n, t, IDX_HEADS, IDX_DIM)
    wi = wi * (IDX_HEADS * IDX_DIM) ** -0.5
    return qa, ka, va, qb, kb, vb, qi, ki, wi


def moba_attend(q, qpos, k_loc, v_loc, kpos_loc, sel_parts, rel_bias_a):
    kvh = jnp.arange(N_HEADS_A) // GROUP_A
    scale = HEAD_DIM ** -0.5
    kl = k_loc[:, :, kvh]
    vl = v_loc[:, :, kvh]
    dist = qpos[:, None] - kpos_loc[None, :]
    ok = (dist >= 0) & (kpos_loc[None, :] >= (qpos // MOBA_BLOCK * MOBA_BLOCK)[:, None])
    bias = jnp.transpose(rel_bias_a[t5_bucket(dist)], (0, 2, 1))
    lo = jnp.einsum('nqhd,nkhd->nqhk', q, kl).astype(F32) * scale
    lo = jnp.where(ok[None, :, None, :], lo + bias[None], -jnp.inf)
    if sel_parts is None:
        p = jax.nn.softmax(lo, axis=-1).astype(vl.dtype)
        return jnp.einsum('nqhk,nkhd->nqhd', p, vl)
    ks, vs, kpos_s, ok_s = sel_parts
    hidx = jnp.arange(N_HEADS_A)[None, None, :, None]
    bias_s = rel_bias_a[t5_bucket(qpos[None, :, None, None] - kpos_s), hidx]
    ls = jnp.einsum('nqhd,nqhkd->nqhk', q, ks).astype(F32) * scale
    ls = jnp.where(ok_s, ls + bias_s, -jnp.inf)
    n_s = ls.shape[-1]
    p = jax.nn.softmax(jnp.concatenate([ls, lo], axis=-1), axis=-1).astype(vl.dtype)
    return (jnp.einsum('nqhk,nqhkd->nqhd', p[..., :n_s], vs)
            + jnp.einsum('nqhk,nkhd->nqhd', p[..., n_s:], vl))


def moba_prompt(q, k, v, rel_bias_a):
    n, t = q.shape[:2]
    nb = -(-t // MOBA_BLOCK)
    pad = nb * MOBA_BLOCK - t
    padw = ((0, 0), (0, pad), (0, 0), (0, 0))
    kblk = jnp.pad(k, padw).reshape(n, nb, MOBA_BLOCK, N_KV_A, HEAD_DIM)
    vblk = jnp.pad(v, padw).reshape(n, nb, MOBA_BLOCK, N_KV_A, HEAD_DIM)
    kvh = jnp.arange(N_HEADS_A) // GROUP_A
    kmean = jnp.mean(kblk.astype(F32), axis=2)[:, :, kvh]
    k_sel = min(MOBA_TOPK, nb - 1)
    bi = jnp.arange(n)[:, None, None, None]
    hi = kvh[None, None, :, None]
    offs = jnp.arange(MOBA_BLOCK)

    def q_block(i):
        q0 = i * Q_BLOCK
        qb = lax.dynamic_slice_in_dim(q, q0, Q_BLOCK, axis=1)
        qpos = q0 + jnp.arange(Q_BLOCK)
        ob = q0 // MOBA_BLOCK
        k_loc = lax.dynamic_index_in_dim(kblk, ob, axis=1, keepdims=False)
        v_loc = lax.dynamic_index_in_dim(vblk, ob, axis=1, keepdims=False)
        kpos_loc = ob * MOBA_BLOCK + offs
        sel_parts = None
        if k_sel > 0:
            own = (qpos // MOBA_BLOCK)[None, :, None, None]
            gate = jnp.einsum('nqhd,nbhd->nqhb', qb.astype(F32), kmean)
            gate = jnp.where(jnp.arange(nb) < own, gate, -jnp.inf)
            _, sel = lax.top_k(gate, k_sel)
            ks = kblk[bi, sel, :, hi].reshape(n, Q_BLOCK, N_HEADS_A, k_sel * MOBA_BLOCK, HEAD_DIM)
            vs = vblk[bi, sel, :, hi].reshape(n, Q_BLOCK, N_HEADS_A, k_sel * MOBA_BLOCK, HEAD_DIM)
            kpos_s = (sel[..., None] * MOBA_BLOCK + offs).reshape(n, Q_BLOCK, N_HEADS_A, -1)
            ok_s = jnp.repeat(sel < own, MOBA_BLOCK, axis=-1)
            sel_parts = (ks, vs, kpos_s, ok_s)
        return moba_attend(qb, qpos, k_loc, v_loc, kpos_loc, sel_parts, rel_bias_a)

    out = lax.map(q_block, jnp.arange(t // Q_BLOCK))
    return jnp.moveaxis(out, 0, 1).reshape(n, t, N_HEADS_A, HEAD_DIM)


def moba_sample(q, k, v, cache_k, cache_v, page_table, rel_bias_a):
    n, s = q.shape[:2]
    nbp = PAST_LEN // MOBA_BLOCK
    ppb = MOBA_BLOCK // PAGE_SIZE
    r = PAST_LEN - nbp * MOBA_BLOCK
    qpos = PAST_LEN + jnp.arange(s)
    own = (qpos // MOBA_BLOCK)[None, :, None, None]
    kvh = jnp.arange(N_HEADS_A) // GROUP_A
    offs = jnp.arange(MOBA_BLOCK)
    k_sel = min(MOBA_TOPK, nbp)
    sel_parts = None
    if k_sel > 0:
        pt = page_table[:, :nbp * ppb].reshape(n, nbp, ppb)
        kmean = jnp.mean(cache_k[pt].astype(F32), axis=(2, 3))[:, :, kvh]
        gate = jnp.einsum('nqhd,nbhd->nqhb', q.astype(F32), kmean)
        gate = jnp.where(jnp.arange(nbp) < own, gate, -jnp.inf)
        _, sel = lax.top_k(gate, k_sel)
        phys = pt[jnp.arange(n)[:, None, None, None], sel]
        hi = kvh[None, None, :, None, None]
        ks = cache_k[phys, :, hi].reshape(n, s, N_HEADS_A, k_sel * MOBA_BLOCK, HEAD_DIM)
        vs = cache_v[phys, :, hi].reshape(n, s, N_HEADS_A, k_sel * MOBA_BLOCK, HEAD_DIM)
        kpos_s = (sel[..., None] * MOBA_BLOCK + offs).reshape(n, s, N_HEADS_A, -1)
        ok_s = jnp.repeat(sel < own, MOBA_BLOCK, axis=-1)
        sel_parts = (ks, vs, kpos_s, ok_s)
    if r > 0:
        pt_own = page_table[:, nbp * ppb: nbp * ppb + r // PAGE_SIZE]
        k_loc = jnp.concatenate([cache_k[pt_own].reshape(n, r, N_KV_A, HEAD_DIM), k], axis=1)
        v_loc = jnp.concatenate([cache_v[pt_own].reshape(n, r, N_KV_A, HEAD_DIM), v], axis=1)
    else:
        k_loc, v_loc = k, v
    kpos_loc = jnp.concatenate([PAST_LEN - r + jnp.arange(r), qpos])
    return moba_attend(q, qpos, k_loc, v_loc, kpos_loc, sel_parts, rel_bias_a)


def sparse_attend(q, ks, vs, qpos, kpos, ok, rel_bias_b):
    n, nq = q.shape[:2]
    nk = ks.shape[2]
    qg = q.reshape(n, nq, N_KV_B, GROUP_B, HEAD_DIM)
    logits = jnp.einsum('nqkgd,nqskd->nqkgs', qg, ks).reshape(n, nq, N_HEADS_B, nk).astype(F32)
    logits = logits * HEAD_DIM ** -0.5
    bias = jnp.swapaxes(rel_bias_b[t5_bucket(qpos[None, :, None] - kpos)], -1, -2)
    logits = jnp.where(ok[:, :, None, :], logits + bias, -jnp.inf)
    p = jax.nn.softmax(logits, axis=-1).astype(vs.dtype).reshape(n, nq, N_KV_B, GROUP_B, nk)
    return jnp.einsum('nqkgs,nqskd->nqkgd', p, vs).reshape(n, nq, N_HEADS_B, HEAD_DIM)


def indexer_scores(qi, wi, ki_all, qpos, kpos):
    raw = jnp.einsum('nqhd,nsd->nqhs', qi.astype(F32), ki_all.astype(F32))
    score = jnp.einsum('nqhs,nqh->nqs', jax.nn.relu(raw), wi.astype(F32))
    return jnp.where(kpos[None, None, :] <= qpos[None, :, None], score, -jnp.inf)


def dsa_prompt(q, k, v, qi, ki, wi, rel_bias_b):
    n, t = q.shape[:2]
    n_sel = min(DSA_TOPK, t // 4)
    kpos = jnp.arange(t)
    bi = jnp.arange(n)[:, None, None]

    def q_block(i):
        q0 = i * Q_BLOCK
        qpos = q0 + jnp.arange(Q_BLOCK)
        qb = lax.dynamic_slice_in_dim(q, q0, Q_BLOCK, axis=1)
        qib = lax.dynamic_slice_in_dim(qi, q0, Q_BLOCK, axis=1)
        wib = lax.dynamic_slice_in_dim(wi, q0, Q_BLOCK, axis=1)
        score = indexer_scores(qib, wib, ki, qpos, kpos)
        _, sel = lax.top_k(score, n_sel)
        ok = sel <= qpos[None, :, None]
        return sparse_attend(qb, k[bi, sel], v[bi, sel], qpos, sel, ok, rel_bias_b)

    out = lax.map(q_block, jnp.arange(t // Q_BLOCK))
    return jnp.moveaxis(out, 0, 1).reshape(n, t, N_HEADS_B, HEAD_DIM)


def dsa_sample(q, k, v, qi, ki, wi, cache_k, cache_v, cache_ki, page_table, rel_bias_b):
    n, s = q.shape[:2]
    total = PAST_LEN + s
    n_sel = min(DSA_TOPK, total // 4)
    qpos = PAST_LEN + jnp.arange(s)
    kpos = jnp.arange(total)
    ki_all = jnp.concatenate([cache_ki[page_table].reshape(n, PAST_LEN, IDX_DIM), ki], axis=1)
    score = indexer_scores(qi, wi, ki_all, qpos, kpos)
    _, sel = lax.top_k(score, n_sel)
    ok = sel <= qpos[None, :, None]
    bi = jnp.arange(n)[:, None, None]
    in_past = (sel < PAST_LEN)[..., None, None]
    sp = jnp.minimum(sel, PAST_LEN - 1)
    phys = page_table[bi, sp // PAGE_SIZE]
    row = sp % PAGE_SIZE
    sn = jnp.clip(sel - PAST_LEN, 0, s - 1)
    ks = jnp.where(in_past, cache_k[phys, row], k[bi, sn])
    vs = jnp.where(in_past, cache_v[phys, row], v[bi, sn])
    return sparse_attend(q, ks, vs, qpos, sel, ok, rel_bias_b)


def merge_out(oa, ob, g_oa, g_ob, w_out):
    n, t = oa.shape[:2]
    oa = rms_norm(oa.reshape(n, t, -1), g_oa)
    ob = rms_norm(ob.reshape(n, t, -1), g_ob)
    return jnp.concatenate([oa, ob], axis=-1) @ w_out


def swiglu(h, w_gate, w_up, w_down):
    return (jax.nn.silu(h @ w_gate) * (h @ w_up)) @ w_down


def setup_inputs(seed: int = 0) -> dict:
    key = jax.random.key(seed)
    ks = jax.random.split(key, 32)
    n_pages = PAST_LEN // PAGE_SIZE
    n_pool = DEC_BATCH * n_pages * POOL_NUM // POOL_DEN
    w_in_width = sum(IN_WIDTHS)

    def nrm(k, shape, s=1.0):
        return jax.random.normal(k, shape, F32) * s

    def gain(k, shape):
        return 1.0 + 0.02 * jax.random.normal(k, shape, F32)

    perm = jax.random.permutation(ks[0], n_pool)
    page_table = perm[:DEC_BATCH * n_pages].reshape(DEC_BATCH, n_pages).astype(jnp.int32)
    return {
        "x_prompt": nrm(ks[1], (BATCH, SEQ, D_MODEL)),
        "x_sample": nrm(ks[2], (DEC_BATCH, DEC_SEQ, D_MODEL)),
        "cache_moba_k": nrm(ks[3], (DEPTH, n_pool, PAGE_SIZE, N_KV_A, HEAD_DIM)),
        "cache_moba_v": nrm(ks[4], (DEPTH, n_pool, PAGE_SIZE, N_KV_A, HEAD_DIM)),
        "cache_dsa_k": nrm(ks[5], (DEPTH, n_pool, PAGE_SIZE, N_KV_B, HEAD_DIM)),
        "cache_dsa_v": nrm(ks[6], (DEPTH, n_pool, PAGE_SIZE, N_KV_B, HEAD_DIM)),
        "cache_idx_k": nrm(ks[7], (DEPTH, n_pool, PAGE_SIZE, IDX_DIM)),
        "page_table": page_table,
        "c_prompt": nrm(ks[8], (BATCH, D_MODEL)),
        "c_sample": nrm(ks[9], (DEC_BATCH, D_MODEL)),
        "rel_bias": nrm(ks[10], (N_BUCKETS, N_HEADS), 0.5),
        "g_attn": gain(ks[11], (DEPTH, D_MODEL)),
        "g_ffn": gain(ks[12], (DEPTH, D_MODEL)),
        "w_ada": nrm(ks[13], (DEPTH, D_MODEL, 6 * D_MODEL), 0.5 * D_MODEL ** -0.5),
        "b_ada": nrm(ks[14], (DEPTH, 6 * D_MODEL), 0.01),
        "w_in": nrm(ks[15], (DEPTH, D_MODEL, w_in_width), D_MODEL ** -0.5),
        "g_qa": gain(ks[16], (DEPTH, HEAD_DIM)),
        "g_ka": gain(ks[17], (DEPTH, HEAD_DIM)),
        "g_qb": gain(ks[18], (DEPTH, HEAD_DIM)),
        "g_kb": gain(ks[19], (DEPTH, HEAD_DIM)),
        "g_oa": gain(ks[20], (DEPTH, N_HEADS_A * HEAD_DIM)),
        "g_ob": gain(ks[21], (DEPTH, N_HEADS_B * HEAD_DIM)),
        "w_out": nrm(ks[22], (DEPTH, N_HEADS * HEAD_DIM, D_MODEL), (N_HEADS * HEAD_DIM) ** -0.5),
        "w_gate": nrm(ks[23], (DEPTH, D_MODEL, D_FF), D_MODEL ** -0.5),
        "w_up": nrm(ks[24], (DEPTH, D_MODEL, D_FF), D_MODEL ** -0.5),
        "w_down": nrm(ks[25], (DEPTH, D_FF, D_MODEL), D_FF ** -0.5),
    }


def reference(x_prompt, x_sample, cache_moba_k, cache_moba_v, cache_dsa_k, cache_dsa_v,
              cache_idx_k, page_table, c_prompt, c_sample, rel_bias, g_attn, g_ffn,
              w_ada, b_ada, w_in, g_qa, g_ka, g_qb, g_kb, g_oa, g_ob, w_out,
              w_gate, w_up, w_down):
    rb_a = rel_bias[:, :N_HEADS_A]
    rb_b = rel_bias[:, N_HEADS_A:]
    xp, xs = x_prompt, x_sample
    mk_p, mv_p, dk_p, dv_p, ik_p = [], [], [], [], []
    mk_s, mv_s, dk_s, dv_s, ik_s = [], [], [], [], []
    for l in range(DEPTH):
        sh1, sc1, gt1, sh2, sc2, gt2 = modulation(c_prompt, w_ada[l], b_ada[l])
        h = modulate(rms_norm(xp, g_attn[l]), sh1, sc1)
        qa, ka, va, qb, kb, vb, qi, ki, wi = project_in(h, w_in[l], g_qa[l], g_ka[l], g_qb[l], g_kb[l])
        oa = moba_prompt(qa, ka, va, rb_a)
        ob = dsa_prompt(qb, kb, vb, qi, ki, wi, rb_b)
        xp = xp + gt1 * merge_out(oa, ob, g_oa[l], g_ob[l], w_out[l])
        xp = xp + gt2 * swiglu(modulate(rms_norm(xp, g_ffn[l]), sh2, sc2), w_gate[l], w_up[l], w_down[l])
        mk_p.append(ka); mv_p.append(va); dk_p.append(kb); dv_p.append(vb); ik_p.append(ki)
        sh1, sc1, gt1, sh2, sc2, gt2 = modulation(c_sample, w_ada[l], b_ada[l])
        h = modulate(rms_norm(xs, g_attn[l]), sh1, sc1)
        qa, ka, va, qb, kb, vb, qi, ki, wi = project_in(h, w_in[l], g_qa[l], g_ka[l], g_qb[l], g_kb[l])
        oa = moba_sample(qa, ka, va, cache_moba_k[l], cache_moba_v[l], page_table, rb_a)
        ob = dsa_sample(qb, kb, vb, qi, ki, wi, cache_dsa_k[l], cache_dsa_v[l], cache_idx_k[l],
                        page_table, rb_b)
        xs = xs + gt1 * merge_out(oa, ob, g_oa[l], g_ob[l], w_out[l])
        xs = xs + gt2 * swiglu(modulate(rms_norm(xs, g_ffn[l]), sh2, sc2), w_gate[l], w_up[l], w_down[l])
        mk_s.append(ka); mv_s.append(va); dk_s.append(kb); dv_s.append(vb); ik_s.append(ki)
    return (xp, xs,
            jnp.stack(mk_p), jnp.stack(mv_p), jnp.stack(dk_p), jnp.stack(dv_p), jnp.stack(ik_p),
            jnp.stack(mk_s), jnp.stack(mv_s), jnp.stack(dk_s), jnp.stack(dv_s), jnp.stack(ik_s))
```

```python
import functools
import math

import numpy as np
import jax
import jax.numpy as jnp
from jax import lax
from jax.experimental import pallas as pl
from jax.experimental.pallas import tpu as pltpu

F32 = jnp.float32
BF16 = jnp.bfloat16

HEAD_DIM = 64
MOBA_BLOCK = 256
MOBA_TOPK = 3
DSA_TOPK = 256
N_BUCKETS = 32
MAX_DISTANCE = 128
EPS = 1e-6
NEG = -1e30
LOWEST = -3.0e38
ROW_TILE = 512
FFN_TILE = 256
TQ = 256
TK = 256
PAGES_PER_STEP = 8
VMEM_LIMIT = 60 * 1024 * 1024


def _cparams(n_axes):
    return pltpu.CompilerParams(dimension_semantics=("arbitrary",) * n_axes,
                                vmem_limit_bytes=VMEM_LIMIT)


def _resident(block_shape, index_map):
    return pl.BlockSpec(block_shape, index_map, pipeline_mode=pl.Buffered(1))


def _dot(a, b):
    return jnp.dot(a, b, preferred_element_type=F32)


def _dot_nt(a, b):
    return lax.dot_general(a, b, (((1,), (1,)), ((), ())), preferred_element_type=F32)


def _bucket_np(dist):
    n = np.maximum(dist, 0)
    exact = N_BUCKETS // 2
    ratio = np.log(np.maximum(n, 1).astype(np.float32) / np.float32(exact)) / np.float32(
        math.log(MAX_DISTANCE / exact))
    large = np.minimum(exact + (ratio * np.float32(N_BUCKETS - exact)).astype(np.int32), N_BUCKETS - 1)
    return np.where(n < exact, n, large).astype(np.int32)


def _mod_kernel(c_ref, w_ref, b_ref, o_ref):
    c = c_ref[...]
    s = c * jax.nn.sigmoid(c)
    o_ref[...] = jnp.dot(s, w_ref[...], preferred_element_type=F32,
                         precision=lax.Precision.HIGHEST) + b_ref[...]


def _modulation(c_all, w_ada, b_ada):
    rows, d = c_all.shape
    n6 = w_ada.shape[1]
    tn = 1024
    return pl.pallas_call(
        _mod_kernel,
        grid=(n6 // tn,),
        in_specs=[pl.BlockSpec((rows, d), lambda j: (0, 0)),
                  pl.BlockSpec((d, tn), lambda j: (0, j)),
                  pl.BlockSpec((1, tn), lambda j: (0, j))],
        out_specs=pl.BlockSpec((rows, tn), lambda j: (0, j)),
        out_shape=jax.ShapeDtypeStruct((rows, n6), F32),
        compiler_params=_cparams(1),
        name="adaln_modulation",
    )(c_all, w_ada, b_ada)


def _bias_kernel(rb_ref, pat_ref, o_ref):
    h = pl.program_id(0)
    pat = pat_ref[...]
    base = rb_ref[N_BUCKETS - 1, h]
    out = jnp.zeros(pat.shape, F32)
    for b in range(N_BUCKETS - 1):
        out = jnp.where(pat == b, rb_ref[b, h] - base, out)
    o_ref[0] = out


def _bias_tiles(rel_bias, pattern):
    n_heads = rel_bias.shape[1]
    nd = pattern.ndim
    zeros = (0,) * nd
    return pl.pallas_call(
        _bias_kernel,
        grid=(n_heads,),
        in_specs=[pl.BlockSpec(memory_space=pltpu.SMEM),
                  pl.BlockSpec(pattern.shape, lambda h: zeros)],
        out_specs=pl.BlockSpec((1,) + pattern.shape, lambda h: (h,) + zeros),
        out_shape=jax.ShapeDtypeStruct((n_heads,) + pattern.shape, F32),
        compiler_params=_cparams(1),
        name="rel_bias_tiles",
    )(rel_bias, jnp.asarray(pattern))


def _prompt_bias_patterns():
    key = np.arange(TK)[:, None]
    qry = np.arange(TQ)[None, :]
    diag = np.where(qry >= key, _bucket_np(qry - key), N_BUCKETS - 1)
    adj = _bucket_np(TK + qry - key)
    return np.stack([diag, adj]).astype(np.int32)


def _sample_bias_patterns(n_new, page):
    s = np.arange(8)[:, None]
    c = np.arange(128)[None, :]
    last = _bucket_np(page + s - c) if page == 128 else None
    assert last is not None
    new = np.where((c <= s) & (c < n_new), _bucket_np(s - c), N_BUCKETS - 1)
    return np.stack([last, new]).astype(np.int32)


def _proj_kernel(x_ref, sh_ref, sc_ref, g_ref, wm_ref, ws_ref, bd_ref, gq_ref, gk_ref,
                 qf_ref, ka_ref, va_ref, kb_ref, vb_ref, sm_ref,
                 kab_ref, kbb_ref, vat_ref, vbt_ref, smb_ref, km_ref, *, wi_scale):
    x = x_ref[...]
    xn = x * lax.rsqrt(jnp.mean(x * x, axis=-1, keepdims=True) + EPS) * g_ref[...]
    h = (xn * (1.0 + sc_ref[...]) + sh_ref[...]).astype(BF16)
    y = _dot(h, wm_ref[...])
    ys = _dot(h, ws_ref[...])

    def head_norm(t, g, width):
        ss = _dot((t * t).astype(BF16), bd_ref[:width, :width])
        return t * lax.rsqrt(ss * (1.0 / HEAD_DIM) + EPS) * g

    scale = HEAD_DIM ** -0.5
    qa = head_norm(y[:, 0:512], gq_ref[0:1, :], 512)
    ka = head_norm(y[:, 512:768], gk_ref[0:1, :], 256)
    va = y[:, 768:1024]
    qb = head_norm(y[:, 1024:1536], gq_ref[1:2, :], 512)
    kb = head_norm(y[:, 1536:1792], gk_ref[1:2, :], 256)
    vb = y[:, 1792:2048]
    qf_ref[:, 0:512] = qa * scale
    qf_ref[:, 512:1024] = qb * scale
    qf_ref[:, 1024:1536] = y[:, 2048:2560]
    ka_ref[...] = ka
    va_ref[...] = va
    kb_ref[...] = kb
    vb_ref[...] = vb
    lane = lax.broadcasted_iota(jnp.int32, (1, 128), 1)
    sm = ys * jnp.where((lane >= 64) & (lane < 72), wi_scale, 1.0)
    sm_ref[...] = sm
    means = []
    for b in range(ROW_TILE // TK):
        rows = slice(b * TK, (b + 1) * TK)
        kab_ref[b] = ka[rows].astype(BF16)
        kbb_ref[b] = kb[rows].astype(BF16)
        vat_ref[b] = va[rows].T.astype(BF16)
        vbt_ref[b] = vb[rows].T.astype(BF16)
        smb_ref[b] = sm[rows].astype(BF16)
        means.append(jnp.mean(ka[rows], axis=0, keepdims=True))
    km_ref[...] = jnp.concatenate(means, axis=0)


def _project(x, shift, scale, g_attn, w_main, w_small, bd, gq, gk, wi_scale):
    rows, d = x.shape
    nt = rows // ROW_TILE
    nb = rows // TK
    bpt = ROW_TILE // TK
    mrows = shift.shape[0]
    mod_map = (lambda i: (i, 0)) if mrows == rows else (lambda i: (0, 0))
    mod_blk = ROW_TILE if mrows == rows else 1
    row_spec = lambda w: pl.BlockSpec((ROW_TILE, w), lambda i: (i, 0))
    blk_spec = lambda w: pl.BlockSpec((bpt, TK, w), lambda i: (i, 0, 0))
    out_shape = (
        jax.ShapeDtypeStruct((rows, 1536), F32),
        jax.ShapeDtypeStruct((rows, 256), F32),
        jax.ShapeDtypeStruct((rows, 256), F32),
        jax.ShapeDtypeStruct((rows, 256), F32),
        jax.ShapeDtypeStruct((rows, 256), F32),
        jax.ShapeDtypeStruct((rows, 128), F32),
        jax.ShapeDtypeStruct((nb, TK, 256), BF16),
        jax.ShapeDtypeStruct((nb, TK, 256), BF16),
        jax.ShapeDtypeStruct((nb, 256, TK), BF16),
        jax.ShapeDtypeStruct((nb, 256, TK), BF16),
        jax.ShapeDtypeStruct((nb, TK, 128), BF16),
        jax.ShapeDtypeStruct((nt, bpt, 256), F32),
    )
    out_specs = (row_spec(1536), row_spec(256), row_spec(256), row_spec(256), row_spec(256),
                 row_spec(128), blk_spec(256), blk_spec(256), blk_spec(TK), blk_spec(TK),
                 blk_spec(128), pl.BlockSpec((None, bpt, 256), lambda i: (i, 0, 0)))
    in_specs = [row_spec(d),
                pl.BlockSpec((mod_blk, d), mod_map),
                pl.BlockSpec((mod_blk, d), mod_map),
                pl.BlockSpec((1, d), lambda i: (0, 0)),
                _resident(w_main.shape, lambda i: (0, 0)),
                _resident(w_small.shape, lambda i: (0, 0)),
                _resident(bd.shape, lambda i: (0, 0)),
                pl.BlockSpec(gq.shape, lambda i: (0, 0)),
                pl.BlockSpec(gk.shape, lambda i: (0, 0))]
    return pl.pallas_call(
        functools.partial(_proj_kernel, wi_scale=wi_scale),
        grid=(nt,), in_specs=in_specs, out_specs=out_specs, out_shape=out_shape,
        compiler_params=_cparams(1), name="adaln_in_projection",
    )(x, shift, scale, g_attn, w_main, w_small, bd, gq, gk)


def _flash_update(s, vt, m, l, acc):
    m_new = jnp.maximum(m, jnp.max(s, axis=0, keepdims=True))
    alpha = jnp.exp(m - m_new)
    p = jnp.exp(s - m_new)
    l = alpha * l + jnp.sum(p, axis=0, keepdims=True)
    acc = alpha * acc + _dot(vt, p.astype(BF16))
    return m_new, l, acc


def _padded_q(q2b, g):
    zeros = jnp.zeros_like(q2b)
    return jnp.concatenate([q2b if gg == g else zeros for gg in range(4)], axis=0)


def _group_q(qT, g):
    return jnp.concatenate([qT[128 * g:128 * g + 64], qT[128 * g + 64:128 * g + 128]], axis=1)


def _attend_group(i, g, qpad, k_ref, vt_ref, bias_ref, far_mask, diag_mask):
    def vt(j):
        return vt_ref[j, 64 * g:64 * g + 64, :]

    def far(j, carry):
        s = _dot(k_ref[j], qpad) + far_mask(j)
        return _flash_update(s, vt(j), *carry)

    def adjacent(j, carry):
        bias = jnp.concatenate([bias_ref[2 * g, 1], bias_ref[2 * g + 1, 1]], axis=1)
        s = _dot(k_ref[j], qpad) + far_mask(j) + bias
        return _flash_update(s, vt(j), *carry)

    carry = (jnp.full((1, 2 * TQ), NEG, F32), jnp.zeros((1, 2 * TQ), F32),
             jnp.zeros((HEAD_DIM, 2 * TQ), F32))
    carry = lax.fori_loop(0, jnp.maximum(i - 1, 0), far, carry)
    carry = lax.fori_loop(jnp.maximum(i - 1, 0), i, adjacent, carry)
    bias = jnp.concatenate([bias_ref[2 * g, 0], bias_ref[2 * g + 1, 0]], axis=1)
    s = diag_mask(_dot(k_ref[i], qpad) + bias)
    m, l, acc = _flash_update(s, vt(i), *carry)
    return acc / l


def _store_group(o_ref, g, out):
    o_ref[128 * g:128 * g + 64, :] = out[:, :TQ]
    o_ref[128 * g + 64:128 * g + 128, :] = out[:, TQ:]


def _moba_kernel(q_ref, k_ref, vt_ref, km_ref, bias_ref, o_ref, mc_ref):
    i = pl.program_id(0)
    qT = q_ref[...].T
    blk = lax.broadcasted_iota(jnp.int32, (128, 2 * TQ), 0).astype(F32)
    valid = blk < i.astype(F32)
    key_io = lax.broadcasted_iota(jnp.int32, (TK, 2 * TQ), 0)
    qry_io = lax.broadcasted_iota(jnp.int32, (TK, 2 * TQ), 1) % TQ
    causal = key_io <= qry_io
    kmb = km_ref[...].astype(BF16)
    for g in range(4):
        qpad = _padded_q(_group_q(qT, g).astype(BF16), g)
        gate = jnp.where(valid, _dot(kmb, qpad), -jnp.inf)
        sel = jnp.zeros_like(gate)
        for _ in range(MOBA_TOPK):
            mx = jnp.max(gate, axis=0, keepdims=True)
            first = jnp.min(jnp.where(gate == mx, blk, 1e9), axis=0, keepdims=True)
            hit = blk == first
            sel = jnp.where(hit & valid, 1.0, sel)
            gate = jnp.where(hit, -jnp.inf, gate)
        mc_ref[...] = jnp.where(sel > 0.0, 0.0, NEG)
        out = _attend_group(i, g, qpad, k_ref, vt_ref, bias_ref,
                            far_mask=lambda j: mc_ref[pl.ds(j, 1), :],
                            diag_mask=lambda s: jnp.where(causal, s, NEG))
        _store_group(o_ref, g, out)


def _moba_prompt(qf, kab, vat, kmean, bias):
    nb = kab.shape[0]
    t = nb * TK
    return pl.pallas_call(
        _moba_kernel,
        grid=(t // TQ,),
        in_specs=[pl.BlockSpec((TQ, 512), lambda i: (i, 0)),
                  _resident(kab.shape, lambda i: (0, 0, 0)),
                  _resident(vat.shape, lambda i: (0, 0, 0)),
                  _resident(kmean.shape, lambda i: (0, 0)),
                  _resident((8, 2, TK, TQ), lambda i: (0, 0, 0, 0))],
        out_specs=pl.BlockSpec((512, TQ), lambda i: (0, i)),
        out_shape=jax.ShapeDtypeStruct((512, t), F32),
        scratch_shapes=[pltpu.VMEM((128, 2 * TQ), F32)],
        compiler_params=_cparams(1), name="moba_prompt_attention",
    )(qf, kab, vat, kmean, bias)


def _dsa_kernel(qb_ref, qi_ref, sm_ref, k_ref, vt_ref, ksm_ref, bias_ref, o_ref, sc_ref,
                *, n_sel, max_iter, idx_iter):
    i = pl.program_id(0)
    qiT = qi_ref[...].T
    zeros = jnp.zeros((64, TQ), F32)
    qi_all = jnp.concatenate(
        [jnp.concatenate([qiT[64 * h:64 * h + 64], zeros], axis=0) for h in range(8)],
        axis=1).astype(BF16)
    wT = sm_ref[...].T
    w_rows = [wT[64 + h:65 + h, :] for h in range(8)]
    key_io = lax.broadcasted_iota(jnp.int32, (TK, TQ), 0)
    qry_io = lax.broadcasted_iota(jnp.int32, (TK, TQ), 1)
    causal = key_io <= qry_io
    key_f = key_io.astype(F32)

    def score_tile(j):
        z = _dot(ksm_ref[j], qi_all)
        sc = w_rows[0] * jnp.maximum(z[:, 0:TQ], 0.0)
        for h in range(1, 8):
            sc = sc + w_rows[h] * jnp.maximum(z[:, TQ * h:TQ * (h + 1)], 0.0)
        return sc

    def fill(j, carry):
        mx, mn = carry
        sc = score_tile(j)
        sc_ref[j] = sc
        return jnp.maximum(mx, sc), jnp.minimum(mn, sc)

    mx, mn = lax.fori_loop(0, i, fill, (jnp.full((TK, TQ), -jnp.inf, F32),
                                         jnp.full((TK, TQ), jnp.inf, F32)))
    scd = score_tile(i)
    sc_ref[i] = jnp.where(causal, scd, -jnp.inf)
    rmax = jnp.max(jnp.maximum(mx, jnp.where(causal, scd, -jnp.inf)), axis=0, keepdims=True)
    rmin = jnp.min(jnp.minimum(mn, jnp.where(causal, scd, jnp.inf)), axis=0, keepdims=True)

    def count_ge(t):
        def body(j, acc):
            return acc + jnp.where(sc_ref[j] >= t, 1.0, 0.0)
        acc = lax.fori_loop(0, i + 1, body, jnp.zeros((TK, TQ), F32))
        return jnp.sum(acc, axis=0, keepdims=True)

    ksel = jnp.float32(n_sel)
    n_adm = (i * TQ + 1).astype(F32) + lax.broadcasted_iota(jnp.int32, (1, TQ), 1).astype(F32)
    few = n_adm <= ksel
    c_top = count_ge(rmax)
    top = (c_top >= ksel) & ~few
    active = jnp.where(few | top, 0.0, 1.0)
    tau = jnp.where(few, LOWEST, rmax)
    tie = jnp.where(top & (c_top > ksel), 1.0, 0.0)
    need = jnp.full((1, TQ), n_sel, F32)

    def cond(st):
        return (st[0] < max_iter) & (jnp.sum(st[1]) > 0.0)

    def step(st):
        it, active, lo, hi, c_hi, tau, tie, need = st
        mid = lo + (hi - lo) * 0.5
        stuck = (mid <= lo) | (mid >= hi)
        c = count_ge(mid)
        act = active > 0.0
        found = act & ~stuck & (c == ksel)
        ended = act & stuck
        ge = c >= ksel
        tau = jnp.where(found, mid, jnp.where(ended, lo, tau))
        tie = jnp.where(ended, 1.0, tie)
        need = jnp.where(ended, ksel - c_hi, need)
        lo = jnp.where(ge, mid, lo)
        hi = jnp.where(ge, hi, mid)
        c_hi = jnp.where(ge, c_hi, c)
        active = jnp.where(found | ended, 0.0, active)
        return it + 1, active, lo, hi, c_hi, tau, tie, need

    st = lax.while_loop(cond, step, (jnp.int32(0), active, rmin, rmax, c_top, tau, tie, need))
    _, active, lo, _, _, tau, tie, need = st
    tau = jnp.where(active > 0.0, lo, tau)

    @pl.when(jnp.sum(tie) > 0.0)
    def _():
        def count_eq_le(jv):
            def body(j, acc):
                idx = key_f + (j * TK).astype(F32)
                return acc + jnp.where((sc_ref[j] == tau) & (idx <= jv), 1.0, 0.0)
            acc = lax.fori_loop(0, i + 1, body, jnp.zeros((TK, TQ), F32))
            return jnp.sum(acc, axis=0, keepdims=True)

        def bis(_, carry):
            lj, hj = carry
            mj = jnp.floor((lj + hj) * 0.5)
            ok = count_eq_le(mj) >= need
            return jnp.where(ok, lj, mj), jnp.where(ok, mj, hj)

        last = ((i + 1) * TK - 1).astype(F32)
        _, hj = lax.fori_loop(0, idx_iter, bis, (jnp.full((1, TQ), -1.0, F32),
                                                 jnp.full((1, TQ), 1.0, F32) * last))

        def drop(j, _):
            idx = key_f + (j * TK).astype(F32)
            t = sc_ref[j]
            sc_ref[j] = jnp.where((tie > 0.0) & (t == tau) & (idx > hj), -jnp.inf, t)
            return 0

        lax.fori_loop(0, i + 1, drop, 0)

    qT = qb_ref[...].T

    def sel_mask(j):
        mb = jnp.where(sc_ref[j] >= tau, 0.0, NEG)
        return jnp.concatenate([mb, mb], axis=1)

    for g in range(4):
        qpad = _padded_q(_group_q(qT, g).astype(BF16), g)
        out = _attend_group(i, g, qpad, k_ref, vt_ref, bias_ref,
                            far_mask=sel_mask, diag_mask=lambda s: s + sel_mask(i))
        _store_group(o_ref, g, out)


def _dsa_prompt(qf, sm, kbb, vbt, smb, bias, n_sel):
    nb = kbb.shape[0]
    t = nb * TK
    kern = functools.partial(_dsa_kernel, n_sel=n_sel, max_iter=256,
                             idx_iter=int(math.ceil(math.log2(t))) + 1)
    return pl.pallas_call(
        kern,
        grid=(t // TQ,),
        in_specs=[pl.BlockSpec((TQ, 512), lambda i: (i, 1)),
                  pl.BlockSpec((TQ, 512), lambda i: (i, 2)),
                  pl.BlockSpec((TQ, 128), lambda i: (i, 0)),
                  _resident(kbb.shape, lambda i: (0, 0, 0)),
                  _resident(vbt.shape, lambda i: (0, 0, 0)),
                  _resident(smb.shape, lambda i: (0, 0, 0)),
                  _resident((8, 2, TK, TQ), lambda i: (1, 0, 0, 0))],
        out_specs=pl.BlockSpec((512, TQ), lambda i: (0, i)),
        out_shape=jax.ShapeDtypeStruct((512, t), F32),
        scratch_shapes=[pltpu.VMEM((nb, TK, TQ), F32)],
        compiler_params=_cparams(1), name="dsa_prompt_attention",
    )(qf, qf, sm, kbb, vbt, smb, bias)


def _ffn_kernel(x_ref, oa_ref, ob_ref, gt1_ref, sh2_ref, sc2_ref, gt2_ref, goa_ref, gob_ref, gf_ref,
                wo_ref, wg_ref, wu_ref, wd_ref, y_ref):
    def feat_norm(o, g):
        return o * lax.rsqrt(jnp.mean(o * o, axis=0, keepdims=True) + EPS) * g

    cat = jnp.concatenate([feat_norm(oa_ref[...], goa_ref[...]),
                           feat_norm(ob_ref[...], gob_ref[...])], axis=0)
    att = _dot(cat.T.astype(BF16), wo_ref[...])
    x1 = x_ref[...] + gt1_ref[...] * att
    xn = x1 * lax.rsqrt(jnp.mean(x1 * x1, axis=-1, keepdims=True) + EPS) * gf_ref[...]
    h = (xn * (1.0 + sc2_ref[...]) + sh2_ref[...]).astype(BF16)
    gate = _dot(h, wg_ref[...])
    up = _dot(h, wu_ref[...])
    act = (gate * jax.nn.sigmoid(gate) * up).astype(BF16)
    y_ref[...] = x1 + gt2_ref[...] * _dot(act, wd_ref[...])


def _out_ffn(x, oat, obt, gt1, sh2, sc2, gt2, goa, gob, g_ffn, wo, wg, wu, wd):
    rows, d = x.shape
    mrows = gt1.shape[0]
    mod_map = (lambda i: (i, 0)) if mrows == rows else (lambda i: (0, 0))
    mod_blk = FFN_TILE if mrows == rows else 1
    mod_spec = pl.BlockSpec((mod_blk, d), mod_map)
    col_spec = pl.BlockSpec((512, FFN_TILE), lambda i: (0, i))
    return pl.pallas_call(
        _ffn_kernel,
        grid=(rows // FFN_TILE,),
        in_specs=[pl.BlockSpec((FFN_TILE, d), lambda i: (i, 0)), col_spec, col_spec,
                  mod_spec, mod_spec, mod_spec, mod_spec,
                  pl.BlockSpec((512, 1), lambda i: (0, 0)),
                  pl.BlockSpec((512, 1), lambda i: (0, 0)),
                  pl.BlockSpec((1, d), lambda i: (0, 0)),
                  _resident(wo.shape, lambda i: (0, 0)),
                  _resident(wg.shape, lambda i: (0, 0)),
                  _resident(wu.shape, lambda i: (0, 0)),
                  _resident(wd.shape, lambda i: (0, 0))],
        out_specs=pl.BlockSpec((FFN_TILE, d), lambda i: (i, 0)),
        out_shape=jax.ShapeDtypeStruct((rows, d), F32),
        compiler_params=_cparams(1), name="out_projection_ffn",
    )(x, oat, obt, gt1, sh2, sc2, gt2, goa, gob, g_ffn, wo, wg, wu, wd)


def _page_specs(width, ppc):
    def spec(p):
        return pl.BlockSpec((None, 128, width), lambda n, c, pt: (pt[n, c * ppc + p], 0, 0))
    return [spec(p) for p in range(ppc)]


def _sample_scan_kernel(pt_ref, *refs, ppc):
    mk = refs[0:ppc]
    ik = refs[ppc:2 * ppc]
    qi_ref, w_ref, km_ref, sc_ref = refs[2 * ppc:]
    qi = qi_ref[...]
    w = w_ref[...]
    sums = []
    for p in range(ppc):
        sums.append(jnp.sum(mk[p][...], axis=0, keepdims=True))
        z = _dot_nt(qi, ik[p][...].astype(BF16))
        zr = jnp.maximum(z, 0.0) * w
        sc = zr[0:8]
        for h in range(1, 8):
            sc = sc + zr[8 * h:8 * h + 8]
        sc_ref[:, 128 * p:128 * (p + 1)] = sc
    ppb = MOBA_BLOCK // 128
    km_ref[...] = jnp.concatenate(
        [(sums[ppb * b] + sums[ppb * b + 1]) * (1.0 / MOBA_BLOCK) for b in range(ppc // ppb)], axis=0)


def _sample_scan(page_table, cache_mk, cache_ik, qi_rows, w_col):
    n, n_pages = page_table.shape
    ppc = PAGES_PER_STEP
    nc = n_pages // ppc
    bpc = ppc * 128 // MOBA_BLOCK
    grid_spec = pltpu.PrefetchScalarGridSpec(
        num_scalar_prefetch=1, grid=(n, nc),
        in_specs=_page_specs(256, ppc) + _page_specs(64, ppc) + [
            pl.BlockSpec((None, 64, 64), lambda n, c, pt: (n, 0, 0)),
            pl.BlockSpec((None, 64, 1), lambda n, c, pt: (n, 0, 0))],
        out_specs=(pl.BlockSpec((None, None, bpc, 256), lambda n, c, pt: (n, c, 0, 0)),
                   pl.BlockSpec((None, 8, ppc * 128), lambda n, c, pt: (n, 0, c))))
    return pl.pallas_call(
        functools.partial(_sample_scan_kernel, ppc=ppc),
        grid_spec=grid_spec,
        out_shape=(jax.ShapeDtypeStruct((n, nc, bpc, 256), F32),
                   jax.ShapeDtypeStruct((n, 8, n_pages * 128), F32)),
        compiler_params=_cparams(2), name="sample_page_scan",
    )(page_table, *([cache_mk] * ppc), *([cache_ik] * ppc), qi_rows, w_col)


def _sample_select_kernel(sc_ref, km_ref, qa_ref, qi_ref, w_ref, kin_ref, bm_ref, mb_ref,
                          *, n_blocks, n_new, n_sel, max_iter, idx_iter):
    lane = lax.broadcasted_iota(jnp.int32, (64, 128), 1).astype(F32)
    valid = lane < float(n_blocks)
    gate = jnp.where(valid, _dot_nt(qa_ref[...], km_ref[...].astype(BF16)), -jnp.inf)
    sel = jnp.zeros_like(gate)
    for _ in range(min(MOBA_TOPK, n_blocks)):
        mx = jnp.max(gate, axis=1, keepdims=True)
        first = jnp.min(jnp.where(gate == mx, lane, 1e9), axis=1, keepdims=True)
        hit = lane == first
        sel = jnp.where(hit & valid, 1.0, sel)
        gate = jnp.where(hit, -jnp.inf, gate)
    bm_ref[...] = jnp.where(sel > 0.0, 0.0, NEG)

    zn = jnp.maximum(_dot_nt(qi_ref[...], kin_ref[...]), 0.0) * w_ref[...]
    new = zn[0:8]
    for h in range(1, 8):
        new = new + zn[8 * h:8 * h + 8]
    tok = lax.broadcasted_iota(jnp.int32, (8, 128), 0)
    col = lax.broadcasted_iota(jnp.int32, (8, 128), 1)
    new = jnp.where((col <= tok) & (col < n_new), new, -jnp.inf)
    a = jnp.concatenate([sc_ref[...], new], axis=1)
    width = a.shape[1]
    idx = lax.broadcasted_iota(jnp.int32, (8, width), 1).astype(F32)
    adm = a > -jnp.inf
    ksel = jnp.float32(n_sel)

    def count(mask):
        return jnp.sum(jnp.where(mask, 1.0, 0.0), axis=1, keepdims=True)

    n_adm = count(adm)
    rmax = jnp.max(a, axis=1, keepdims=True)
    rmin = jnp.min(jnp.where(adm, a, jnp.inf), axis=1, keepdims=True)
    few = n_adm <= ksel
    c_top = count(a >= rmax)
    top = (c_top >= ksel) & ~few
    active = jnp.where(few | top, 0.0, 1.0)
    tau = jnp.where(few, LOWEST, rmax)
    tie = jnp.where(top & (c_top > ksel), 1.0, 0.0)
    need = jnp.full((8, 1), n_sel, F32)

    def cond(st):
        return (st[0] < max_iter) & (jnp.sum(st[1]) > 0.0)

    def step(st):
        it, active, lo, hi, c_hi, tau, tie, need = st
        mid = lo + (hi - lo) * 0.5
        stuck = (mid <= lo) | (mid >= hi)
        c = count(a >= mid)
        act = active > 0.0
        found = act & ~stuck & (c == ksel)
        ended = act & stuck
        ge = c >= ksel
        tau = jnp.where(found, mid, jnp.where(ended, lo, tau))
        tie = jnp.where(ended, 1.0, tie)
        need = jnp.where(ended, ksel - c_hi, need)
        lo = jnp.where(ge, mid, lo)
        hi = jnp.where(ge, hi, mid)
        c_hi = jnp.where(ge, c_hi, c)
        active = jnp.where(found | ended, 0.0, active)
        return it + 1, active, lo, hi, c_hi, tau, tie, need

    st = lax.while_loop(cond, step, (jnp.int32(0), active, rmin, rmax, c_top, tau, tie, need))
    _, active, lo, _, _, tau, tie, need = st
    tau = jnp.where(active > 0.0, lo, tau)

    eq = a == tau

    def bis(_, carry):
        lj, hj = carry
        mj = jnp.floor((lj + hj) * 0.5)
        ok = count(eq & (idx <= mj)) >= need
        return jnp.where(ok, lj, mj), jnp.where(ok, mj, hj)

    _, hj = lax.fori_loop(0, idx_iter, bis, (jnp.full((8, 1), -1.0, F32),
                                             jnp.full((8, 1), float(width - 1), F32)))
    hj = jnp.where(tie > 0.0, hj, float(width))
    keep = (a > tau) | (eq & (idx <= hj))
    mb_ref[...] = jnp.where(keep, 0.0, NEG)


def _sample_select(scores, kmean, qa_rows, qi_rows, w_col, ki_new, n_blocks, n_new, n_sel):
    n, _, past = scores.shape
    width = past + 128
    kern = functools.partial(_sample_select_kernel, n_blocks=n_blocks, n_new=n_new, n_sel=n_sel,
                             max_iter=256, idx_iter=int(math.ceil(math.log2(width))) + 1)
    per_n = lambda *shape: pl.BlockSpec((None,) + shape, lambda b: (b,) + (0,) * len(shape))
    return pl.pallas_call(
        kern, grid=(n,),
        in_specs=[per_n(8, past), per_n(128, 256), per_n(64, 256), per_n(64, 64), per_n(64, 1),
                  per_n(128, 64)],
        out_specs=(per_n(64, 128), per_n(8, width)),
        out_shape=(jax.ShapeDtypeStruct((n, 64, 128), F32),
                   jax.ShapeDtypeStruct((n, 8, width), F32)),
        compiler_params=_cparams(1), name="sample_selection",
    )(scores, kmean, qa_rows, qi_rows, w_col, ki_new)


def _row_update(s, v, m_ref, l_ref, acc_ref):
    m = m_ref[...]
    m_new = jnp.maximum(m, jnp.max(s, axis=1, keepdims=True))
    alpha = jnp.exp(m - m_new)
    p = jnp.exp(s - m_new)
    l_ref[...] = alpha * l_ref[...] + jnp.sum(p, axis=1, keepdims=True)
    acc_ref[...] = alpha * acc_ref[...] + _dot(p.astype(BF16), v)
    m_ref[...] = m_new


def _sample_attn_kernel(pt_ref, *refs, ppc, n_new):
    mk = refs[0:ppc]
    mv = refs[ppc:2 * ppc]
    dk = refs[2 * ppc:3 * ppc]
    dv = refs[3 * ppc:4 * ppc]
    (qa_ref, qb_ref, bm_ref, mbc_ref, mbn_ref, bl_ref, bn_ref, kan_ref, van_ref, kbn_ref, vbn_ref,
     oa_ref, ob_ref, ma_ref, la_ref, acca_ref, mb_ref, lb_ref, accb_ref) = refs[4 * ppc:]
    c = pl.program_id(1)
    last = c == pl.num_programs(1) - 1

    @pl.when(c == 0)
    def _():
        for m_ref, l_ref, acc_ref in ((ma_ref, la_ref, acca_ref), (mb_ref, lb_ref, accb_ref)):
            m_ref[...] = jnp.full(m_ref.shape, NEG, F32)
            l_ref[...] = jnp.zeros(l_ref.shape, F32)
            acc_ref[...] = jnp.zeros(acc_ref.shape, F32)

    qa = qa_ref[...]
    qb = qb_ref[...]
    bm = bm_ref[...]
    lane = lax.broadcasted_iota(jnp.int32, (64, 128), 1)
    ppb = MOBA_BLOCK // 128
    tile8 = lambda x: jnp.concatenate([x] * 8, axis=0)
    for p in range(ppc):
        blk = c * (ppc // ppb) + p // ppb
        s = _dot_nt(qa, mk[p][...].astype(BF16))
        s = s + jnp.sum(jnp.where(lane == blk, bm, 0.0), axis=1, keepdims=True)
        if p == ppc - 1:
            s = s + jnp.where(last, bl_ref[0], 0.0)
        _row_update(s, mv[p][...].astype(BF16), ma_ref, la_ref, acca_ref)
        s = _dot_nt(qb, dk[p][...].astype(BF16)) + tile8(mbc_ref[:, 128 * p:128 * (p + 1)])
        if p == ppc - 1:
            s = s + jnp.where(last, bl_ref[1], 0.0)
        _row_update(s, dv[p][...].astype(BF16), mb_ref, lb_ref, accb_ref)

    @pl.when(last)
    def _():
        tok = lax.broadcasted_iota(jnp.int32, (64, 128), 0) % 8
        own = (lane <= tok) & (lane < n_new)
        s = jnp.where(own, _dot_nt(qa, kan_ref[...]) + bn_ref[0], NEG)
        _row_update(s, van_ref[...], ma_ref, la_ref, acca_ref)
        s = _dot_nt(qb, kbn_ref[...]) + bn_ref[1] + tile8(mbn_ref[...])
        _row_update(s, vbn_ref[...], mb_ref, lb_ref, accb_ref)
        grp = lax.broadcasted_iota(jnp.int32, (64, 64), 0) // 16
        for o_ref, l_ref, acc_ref in ((oa_ref, la_ref, acca_ref), (ob_ref, lb_ref, accb_ref)):
            o = acc_ref[...] / l_ref[...]
            out = jnp.zeros((64, 64), F32)
            for g in range(4):
                out = out + jnp.where(grp == g, o[:, 64 * g:64 * g + 64], 0.0)
            o_ref[...] = out


def _sample_attend(page_table, cache_mk, cache_mv, cache_dk, cache_dv, qa_rows, qb_rows, bm, mb,
                   bias_last, bias_new, ka_new, va_new, kb_new, vb_new, n_new):
    n, n_pages = page_table.shape
    ppc = PAGES_PER_STEP
    nc = n_pages // ppc
    per_n = lambda *shape: pl.BlockSpec((None,) + shape, lambda b, c, pt: (b,) + (0,) * len(shape))
    const = lambda *shape: pl.BlockSpec(shape, lambda b, c, pt: (0,) * len(shape))
    in_specs = (_page_specs(256, ppc) * 4 + [
        per_n(64, 256), per_n(64, 256), per_n(64, 128),
        pl.BlockSpec((None, 8, ppc * 128), lambda b, c, pt: (b, 0, c)),
        pl.BlockSpec((None, 8, 128), lambda b, c, pt: (b, 0, n_pages)),
        const(2, 64, 128), const(2, 64, 128),
        per_n(128, 256), per_n(128, 256), per_n(128, 256), per_n(128, 256)])
    grid_spec = pltpu.PrefetchScalarGridSpec(
        num_scalar_prefetch=1, grid=(n, nc), in_specs=in_specs,
        out_specs=(per_n(64, 64), per_n(64, 64)),
        scratch_shapes=[pltpu.VMEM((64, 1), F32), pltpu.VMEM((64, 1), F32), pltpu.VMEM((64, 256), F32),
                        pltpu.VMEM((64, 1), F32), pltpu.VMEM((64, 1), F32), pltpu.VMEM((64, 256), F32)])
    return pl.pallas_call(
        functools.partial(_sample_attn_kernel, ppc=ppc, n_new=n_new),
        grid_spec=grid_spec,
        out_shape=(jax.ShapeDtypeStruct((n, 64, 64), F32), jax.ShapeDtypeStruct((n, 64, 64), F32)),
        compiler_params=_cparams(2), name="sample_paged_attention",
    )(page_table, *([cache_mk] * ppc), *([cache_mv] * ppc), *([cache_dk] * ppc), *([cache_dv] * ppc),
      qa_rows, qb_rows, bm, mb, mb, bias_last, bias_new, ka_new, va_new, kb_new, vb_new)


def _head_token_rows(q, n, s):
    q = q.reshape(n, s, 8, HEAD_DIM).transpose(0, 2, 1, 3)
    q = jnp.pad(q, ((0, 0), (0, 0), (0, 8 - s), (0, 0)))
    return q.reshape(n, 64, HEAD_DIM)


def _block_diag_rows(q_rows):
    n = q_rows.shape[0]
    grp = jnp.arange(64) // 16
    onehot = (grp[:, None] == jnp.arange(4)[None, :]).astype(q_rows.dtype)
    return (q_rows[:, :, None, :] * onehot[None, :, :, None]).reshape(n, 64, 256)


def _pad_new_rows(x, n, s):
    w = x.shape[1]
    return jnp.pad(x.reshape(n, s, w), ((0, 0), (0, 128 - s), (0, 0))).astype(BF16)


def kernel(x_prompt, x_sample, cache_moba_k, cache_moba_v, cache_dsa_k, cache_dsa_v, cache_idx_k,
           page_table, c_prompt, c_sample, rel_bias, g_attn, g_ffn, w_ada, b_ada, w_in, g_qa, g_ka,
           g_qb, g_kb, g_oa, g_ob, w_out, w_gate, w_up, w_down):
    batch, seq, d = x_prompt.shape
    n_dec, s_dec, _ = x_sample.shape
    depth = w_in.shape[0]
    n_pool, page = cache_idx_k.shape[1], cache_idx_k.shape[2]
    n_pages = page_table.shape[1]
    past = n_pages * page
    assert batch == 1 and d == 1024 and rel_bias.shape == (N_BUCKETS, 16)
    assert seq % ROW_TILE == 0 and (n_dec * s_dec) % ROW_TILE == 0 and s_dec <= 8
    assert page == 128 and past % MOBA_BLOCK == 0 and n_pages % PAGES_PER_STEP == 0
    assert seq // MOBA_BLOCK <= 128 and past // MOBA_BLOCK <= 128
    rows_s = n_dec * s_dec
    n_sel_p = min(DSA_TOPK, seq // 4)
    n_sel_s = min(DSA_TOPK, (past + s_dec) // 4)
    idx_width = 512
    wi_scale = float(idx_width) ** -0.5

    bias_p = _bias_tiles(rel_bias, _prompt_bias_patterns())
    bias_s = _bias_tiles(rel_bias, _sample_bias_patterns(s_dec, page))
    bias_last = bias_s[:, 0].reshape(2, 64, 128)
    bias_new = bias_s[:, 1].reshape(2, 64, 128)
    bd = jnp.asarray(np.kron(np.eye(8, dtype=np.float32), np.ones((64, 64), np.float32)), BF16)

    xp = x_prompt.reshape(seq, d)
    xs = x_sample.reshape(rows_s, d)
    pad_rows = (-(1 + n_dec)) % 8
    c_all = jnp.concatenate([c_prompt, c_sample, jnp.zeros((pad_rows, d), F32)], axis=0)
    outs_p, outs_s = [], []
    for l in range(depth):
        mod = _modulation(c_all, w_ada[l], b_ada[l][None, :])
        mod_p = [mod[0:1, k * d:(k + 1) * d] for k in range(6)]
        mod_s = [jnp.repeat(mod[1:1 + n_dec, k * d:(k + 1) * d], s_dec, axis=0) for k in range(6)]
        w = w_in[l]
        w_main = jnp.concatenate([w[:, 0:1536], w[:, 1536:2048], w[:, 2048:2560]], axis=1).astype(BF16)
        w_small = jnp.pad(w[:, 2560:2632], ((0, 0), (0, 56))).astype(BF16)
        gq = jnp.stack([jnp.tile(g_qa[l], 8), jnp.tile(g_qb[l], 8)])
        gk = jnp.stack([jnp.tile(g_ka[l], 4), jnp.tile(g_kb[l], 4)])
        goa, gob = g_oa[l][:, None], g_ob[l][:, None]
        wo, wg, wu, wd = (w_out[l].astype(BF16), w_gate[l].astype(BF16), w_up[l].astype(BF16),
                          w_down[l].astype(BF16))

        (qf, ka, va, kb, vb, sm, kab, kbb, vat, vbt, smb, km) = _project(
            xp, mod_p[0], mod_p[1], g_attn[l][None, :], w_main, w_small, bd, gq, gk, wi_scale)
        nb = seq // MOBA_BLOCK
        kmean = jnp.pad(km.reshape(nb, 256), ((0, 128 - nb), (0, 0)))
        oat = _moba_prompt(qf, kab, vat, kmean, bias_p)
        obt = _dsa_prompt(qf, sm, kbb, vbt, smb, bias_p, n_sel_p)
        xp = _out_ffn(xp, oat, obt, mod_p[2], mod_p[3], mod_p[4], mod_p[5], goa, gob,
                      g_ffn[l][None, :], wo, wg, wu, wd)
        outs_p.append((ka.reshape(1, seq, 4, 64), va.reshape(1, seq, 4, 64), kb.reshape(1, seq, 4, 64),
                       vb.reshape(1, seq, 4, 64), sm[:, :64].reshape(1, seq, 64)))

        (qf, ka, va, kb, vb, sm, _, _, _, _, _, _) = _project(
            xs, mod_s[0], mod_s[1], g_attn[l][None, :], w_main, w_small, bd, gq, gk, wi_scale)
        qa_rows = _block_diag_rows(_head_token_rows(qf[:, 0:512], n_dec, s_dec)).astype(BF16)
        qb_rows = _block_diag_rows(_head_token_rows(qf[:, 512:1024], n_dec, s_dec)).astype(BF16)
        qi_rows = _head_token_rows(qf[:, 1024:1536], n_dec, s_dec).astype(BF16)
        w_col = jnp.pad(sm[:, 64:72].reshape(n_dec, s_dec, 8).transpose(0, 2, 1),
                        ((0, 0), (0, 0), (0, 8 - s_dec))).reshape(n_dec, 64, 1)
        cmk = cache_moba_k[l].reshape(n_pool, page, 256)
        cmv = cache_moba_v[l].reshape(n_pool, page, 256)
        cdk = cache_dsa_k[l].reshape(n_pool, page, 256)
        cdv = cache_dsa_v[l].reshape(n_pool, page, 256)
        km_s, scores = _sample_scan(page_table, cmk, cache_idx_k[l], qi_rows, w_col)
        nbp = past // MOBA_BLOCK
        km_s = jnp.pad(km_s.reshape(n_dec, nbp, 256), ((0, 0), (0, 128 - nbp), (0, 0)))
        bm, mb = _sample_select(scores, km_s, qa_rows, qi_rows, w_col,
                                _pad_new_rows(sm[:, :64], n_dec, s_dec), nbp, s_dec, n_sel_s)
        oa_s, ob_s = _sample_attend(
            page_table, cmk, cmv, cdk, cdv, qa_rows, qb_rows, bm, mb, bias_last, bias_new,
            _pad_new_rows(ka, n_dec, s_dec), _pad_new_rows(va, n_dec, s_dec),
            _pad_new_rows(kb, n_dec, s_dec), _pad_new_rows(vb, n_dec, s_dec), s_dec)
        to_t = lambda o: o.reshape(n_dec, 8, 8, 64)[:, :, :s_dec].transpose(1, 3, 0, 2).reshape(512, rows_s)
        xs = _out_ffn(xs, to_t(oa_s), to_t(ob_s), mod_s[2], mod_s[3], mod_s[4], mod_s[5], goa, gob,
                      g_ffn[l][None, :], wo, wg, wu, wd)
        outs_s.append((ka.reshape(n_dec, s_dec, 4, 64), va.reshape(n_dec, s_dec, 4, 64),
                       kb.reshape(n_dec, s_dec, 4, 64), vb.reshape(n_dec, s_dec, 4, 64),
                       sm[:, :64].reshape(n_dec, s_dec, 64)))

    stack = lambda outs, k: jnp.stack([o[k] for o in outs])
    return (xp.reshape(batch, seq, d), xs.reshape(n_dec, s_dec, d),
            *[stack(outs_p, k) for k in range(5)], *[stack(outs_s, k) for k in range(5)])
```

```python
import functools
import math

import numpy as np
import jax
import jax.numpy as jnp
from jax import lax
from jax.experimental import pallas as pl
from jax.experimental.pallas import tpu as pltpu

F32 = jnp.float32
BF16 = jnp.bfloat16

HEAD_DIM = 64
MOBA_BLOCK = 256
MOBA_TOPK = 3
DSA_TOPK = 256
N_BUCKETS = 32
MAX_DISTANCE = 128
EPS = 1e-6
LOG2E = math.log2(math.e)
NEG = -1e30
LOWEST = -3.0e38
ROW_TILE = 512
FFN_TILE = 256
TQ = 256
TK = 256
PAGES_PER_STEP = 8
SELECT_BATCH = 8
MAX_BISECT = 256
VMEM_LIMIT = 60 * 1024 * 1024


def _cparams(n_axes):
    return pltpu.CompilerParams(dimension_semantics=("arbitrary",) * n_axes,
                                vmem_limit_bytes=VMEM_LIMIT)


def _resident(block_shape, index_map):
    return pl.BlockSpec(block_shape, index_map, pipeline_mode=pl.Buffered(1))


def _dot(a, b):
    return jnp.dot(a, b, preferred_element_type=F32)


def _dot_nt(a, b):
    return lax.dot_general(a, b, (((1,), (1,)), ((), ())), preferred_element_type=F32)


def _bucket_np(dist):
    n = np.maximum(dist, 0)
    exact = N_BUCKETS // 2
    ratio = np.log(np.maximum(n, 1).astype(np.float32) / np.float32(exact)) / np.float32(
        math.log(MAX_DISTANCE / exact))
    large = np.minimum(exact + (ratio * np.float32(N_BUCKETS - exact)).astype(np.int32), N_BUCKETS - 1)
    return np.where(n < exact, n, large).astype(np.int32)


def _mod_kernel(c_ref, w_ref, b_ref, o_ref):
    c = c_ref[...]
    s = c * jax.nn.sigmoid(c)
    o_ref[...] = jnp.dot(s, w_ref[...], preferred_element_type=F32,
                         precision=lax.Precision.HIGHEST) + b_ref[...]


def _modulation(c_all, w_ada, b_ada):
    rows, d = c_all.shape
    n6 = w_ada.shape[1]
    tn = 1024
    return pl.pallas_call(
        _mod_kernel,
        grid=(n6 // tn,),
        in_specs=[pl.BlockSpec((rows, d), lambda j: (0, 0)),
                  pl.BlockSpec((d, tn), lambda j: (0, j)),
                  pl.BlockSpec((1, tn), lambda j: (0, j))],
        out_specs=pl.BlockSpec((rows, tn), lambda j: (0, j)),
        out_shape=jax.ShapeDtypeStruct((rows, n6), F32),
        compiler_params=_cparams(1),
        name="adaln_modulation",
    )(c_all, w_ada, b_ada)


def _bias_kernel(rb_ref, pat_ref, o_ref):
    h = pl.program_id(0)
    pat = pat_ref[...]
    base = rb_ref[N_BUCKETS - 1, h]
    out = jnp.zeros(pat.shape, F32)
    for b in range(N_BUCKETS - 1):
        out = jnp.where(pat == b, rb_ref[b, h] - base, out)
    o_ref[0] = out * LOG2E


def _bias_tiles(rel_bias, pattern):
    n_heads = rel_bias.shape[1]
    nd = pattern.ndim
    zeros = (0,) * nd
    return pl.pallas_call(
        _bias_kernel,
        grid=(n_heads,),
        in_specs=[pl.BlockSpec(memory_space=pltpu.SMEM),
                  pl.BlockSpec(pattern.shape, lambda h: zeros)],
        out_specs=pl.BlockSpec((1,) + pattern.shape, lambda h: (h,) + zeros),
        out_shape=jax.ShapeDtypeStruct((n_heads,) + pattern.shape, F32),
        compiler_params=_cparams(1),
        name="rel_bias_tiles",
    )(rel_bias, jnp.asarray(pattern))


def _prompt_bias_patterns():
    key = np.arange(TK)[:, None]
    qry = np.arange(TQ)[None, :]
    diag = np.where(qry >= key, _bucket_np(qry - key), N_BUCKETS - 1)
    adj = _bucket_np(TK + qry - key)
    return np.stack([diag, adj]).astype(np.int32)


def _sample_bias_patterns(n_new, page):
    s = np.arange(8)[:, None]
    c = np.arange(page)[None, :]
    last = _bucket_np(page + s - c)
    new = np.where((c <= s) & (c < n_new), _bucket_np(s - c), N_BUCKETS - 1)
    return np.stack([last, new]).astype(np.int32)


def _proj_kernel(x_ref, sh_ref, sc_ref, g_ref, wm_ref, ws_ref, bd_ref, gq_ref, gk_ref,
                 qf_ref, ka_ref, va_ref, kb_ref, vb_ref, sm_ref,
                 kab_ref, kbb_ref, vat_ref, vbt_ref, smb_ref, km_ref, *, wi_scale):
    x = x_ref[...]
    xn = x * lax.rsqrt(jnp.mean(x * x, axis=-1, keepdims=True) + EPS) * g_ref[...]
    h = (xn * (1.0 + sc_ref[...]) + sh_ref[...]).astype(BF16)
    y = _dot(h, wm_ref[...])
    ys = _dot(h, ws_ref[...])

    def head_norm(t, g, width):
        ss = _dot((t * t).astype(BF16), bd_ref[:width, :width])
        return t * lax.rsqrt(ss * (1.0 / HEAD_DIM) + EPS) * g

    scale = HEAD_DIM ** -0.5 * LOG2E
    qa = head_norm(y[:, 0:512], gq_ref[0:1, :], 512)
    ka = head_norm(y[:, 512:768], gk_ref[0:1, :], 256)
    va = y[:, 768:1024]
    qb = head_norm(y[:, 1024:1536], gq_ref[1:2, :], 512)
    kb = head_norm(y[:, 1536:1792], gk_ref[1:2, :], 256)
    vb = y[:, 1792:2048]
    qf_ref[:, 0:512] = qa * scale
    qf_ref[:, 512:1024] = qb * scale
    qf_ref[:, 1024:1536] = y[:, 2048:2560]
    ka_ref[...] = ka
    va_ref[...] = va
    kb_ref[...] = kb
    vb_ref[...] = vb
    lane = lax.broadcasted_iota(jnp.int32, (1, 128), 1)
    sm = ys * jnp.where((lane >= 64) & (lane < 72), wi_scale, 1.0)
    sm_ref[...] = sm
    means = []
    for b in range(ROW_TILE // TK):
        rows = slice(b * TK, (b + 1) * TK)
        kab_ref[b] = ka[rows].astype(BF16)
        kbb_ref[b] = kb[rows].astype(BF16)
        vat_ref[b] = va[rows].T.astype(BF16)
        vbt_ref[b] = vb[rows].T.astype(BF16)
        smb_ref[b] = sm[rows].astype(BF16)
        means.append(jnp.mean(ka[rows], axis=0, keepdims=True))
    km_ref[...] = jnp.concatenate(means, axis=0)


def _project(x, shift, scale, g_attn, w_main, w_small, bd, gq, gk, wi_scale):
    rows, d = x.shape
    nt = rows // ROW_TILE
    nb = rows // TK
    bpt = ROW_TILE // TK
    mrows = shift.shape[0]
    mod_map = (lambda i: (i, 0)) if mrows == rows else (lambda i: (0, 0))
    mod_blk = ROW_TILE if mrows == rows else 1
    row_spec = lambda w: pl.BlockSpec((ROW_TILE, w), lambda i: (i, 0))
    blk_spec = lambda w: pl.BlockSpec((bpt, TK, w), lambda i: (i, 0, 0))
    out_shape = (
        jax.ShapeDtypeStruct((rows, 1536), F32),
        jax.ShapeDtypeStruct((rows, 256), F32),
        jax.ShapeDtypeStruct((rows, 256), F32),
        jax.ShapeDtypeStruct((rows, 256), F32),
        jax.ShapeDtypeStruct((rows, 256), F32),
        jax.ShapeDtypeStruct((rows, 128), F32),
        jax.ShapeDtypeStruct((nb, TK, 256), BF16),
        jax.ShapeDtypeStruct((nb, TK, 256), BF16),
        jax.ShapeDtypeStruct((nb, 256, TK), BF16),
        jax.ShapeDtypeStruct((nb, 256, TK), BF16),
        jax.ShapeDtypeStruct((nb, TK, 128), BF16),
        jax.ShapeDtypeStruct((nt, bpt, 256), F32),
    )
    out_specs = (row_spec(1536), row_spec(256), row_spec(256), row_spec(256), row_spec(256),
                 row_spec(128), blk_spec(256), blk_spec(256), blk_spec(TK), blk_spec(TK),
                 blk_spec(128), pl.BlockSpec((None, bpt, 256), lambda i: (i, 0, 0)))
    in_specs = [row_spec(d),
                pl.BlockSpec((mod_blk, d), mod_map),
                pl.BlockSpec((mod_blk, d), mod_map),
                pl.BlockSpec((1, d), lambda i: (0, 0)),
                _resident(w_main.shape, lambda i: (0, 0)),
                _resident(w_small.shape, lambda i: (0, 0)),
                _resident(bd.shape, lambda i: (0, 0)),
                pl.BlockSpec(gq.shape, lambda i: (0, 0)),
                pl.BlockSpec(gk.shape, lambda i: (0, 0))]
    return pl.pallas_call(
        functools.partial(_proj_kernel, wi_scale=wi_scale),
        grid=(nt,), in_specs=in_specs, out_specs=out_specs, out_shape=out_shape,
        compiler_params=_cparams(1), name="adaln_in_projection",
    )(x, shift, scale, g_attn, w_main, w_small, bd, gq, gk)


def _flash_update(s, vt, m, l, acc, after=None):
    m_new = jnp.maximum(m, jnp.max(s, axis=0, keepdims=True))
    if after is not None:
        m_new = jnp.maximum(m_new, after[0:1, :] * 0.0 + NEG)
    alpha = jnp.exp2(m - m_new)
    p = jnp.exp2(s - m_new)
    l = alpha * l + jnp.sum(p, axis=0, keepdims=True)
    acc = alpha * acc + _dot(vt, p.astype(BF16))
    return m_new, l, acc


def _store_padded_queries(qp_ref, q_ref):
    qT = q_ref[...].T
    zeros = jnp.zeros((64, 2 * TQ), BF16)
    for g in range(4):
        q2 = jnp.concatenate([qT[128 * g:128 * g + 64], qT[128 * g + 64:128 * g + 128]],
                             axis=1).astype(BF16)
        qp_ref[g] = jnp.concatenate([q2 if gg == g else zeros for gg in range(4)], axis=0)


def _attend(i, qp_ref, k_ref, vt_ref, bias_ref, o_ref, far_masks, diag_masks):
    def tile_update(j, carry, masks, near):
        kt = k_ref[j]
        logits = [_dot(kt, qp_ref[g]) for g in range(4)]
        out = []
        for g in range(4):
            s = logits[g] + masks[g]
            if near is not None:
                s = s + jnp.concatenate([bias_ref[2 * g, near], bias_ref[2 * g + 1, near]], axis=1)
            out.extend(_flash_update(s, vt_ref[j, 64 * g:64 * g + 64, :], *carry[3 * g:3 * g + 3],
                                     after=logits[g + 1] if g < 3 else None))
        return tuple(out)

    carry = (jnp.full((1, 2 * TQ), NEG, F32), jnp.zeros((1, 2 * TQ), F32),
             jnp.zeros((HEAD_DIM, 2 * TQ), F32)) * 4
    n_far = jnp.maximum(i - 1, 0)
    carry = lax.fori_loop(0, n_far, lambda j, c: tile_update(j, c, far_masks(j), None), carry)
    carry = lax.fori_loop(n_far, i, lambda j, c: tile_update(j, c, far_masks(j), 1), carry)
    carry = tile_update(i, carry, diag_masks(), 0)
    for g in range(4):
        out = carry[3 * g + 2] / carry[3 * g + 1]
        o_ref[128 * g:128 * g + 64, :] = out[:, :TQ]
        o_ref[128 * g + 64:128 * g + 128, :] = out[:, TQ:]


def _kth_largest(count_ge, counts_init, rmin, rmax, n_adm, n_sel):
    ksel = jnp.float32(n_sel)
    c_top, c_zero, c_pos = counts_init
    few = n_adm <= ksel
    top = ~few & (c_top >= ksel)
    zero = ~few & ~top & (c_pos < ksel) & (c_zero >= ksel)
    pos = c_pos >= ksel
    active = jnp.where(few | top | zero, 0.0, 1.0)
    tau = jnp.where(few, LOWEST, jnp.where(top, rmax, 0.0))
    check = jnp.where(top | zero, 1.0, 0.0)
    lo = jnp.where(pos, 0.0, rmin)
    hi = jnp.where(pos, rmax, 0.0)

    def cond(st):
        return (st[0] < MAX_BISECT) & (jnp.sum(st[1]) > 0.0)

    def step(st):
        it, active, lo, hi, tau, check = st
        mid = lo + (hi - lo) * 0.5
        stuck = (mid <= lo) | (mid >= hi)
        c = count_ge(mid)
        act = active > 0.0
        found = act & ~stuck & (c == ksel)
        ended = act & stuck
        ge = c >= ksel
        tau = jnp.where(found, mid, jnp.where(ended, lo, tau))
        check = jnp.where(ended, 1.0, check)
        lo = jnp.where(ge, mid, lo)
        hi = jnp.where(ge, hi, mid)
        active = jnp.where(found | ended, 0.0, active)
        return it + 1, active, lo, hi, tau, check

    _, active, lo, _, tau, check = lax.while_loop(
        cond, step, (jnp.int32(0), active, lo, hi, tau, check))
    left = active > 0.0
    return jnp.where(left, lo, tau), jnp.where(left, 1.0, check)


def _moba_kernel(q_ref, k_ref, vt_ref, km_ref, bias_ref, o_ref, qp_ref, mc_ref):
    i = pl.program_id(0)
    _store_padded_queries(qp_ref, q_ref)
    blk = lax.broadcasted_iota(jnp.int32, (128, 2 * TQ), 0).astype(F32)
    valid = blk < i.astype(F32)
    kmb = km_ref[...].astype(BF16)
    for g in range(4):
        gate = jnp.where(valid, _dot(kmb, qp_ref[g]), -jnp.inf)
        sel = jnp.zeros_like(gate)
        for _ in range(MOBA_TOPK):
            mx = jnp.max(gate, axis=0, keepdims=True)
            first = jnp.min(jnp.where(gate == mx, blk, 1e9), axis=0, keepdims=True)
            hit = blk == first
            sel = jnp.where(hit & valid, 1.0, sel)
            gate = jnp.where(hit, -jnp.inf, gate)
        mc_ref[g] = jnp.where(sel > 0.0, 0.0, NEG)
    key_io = lax.broadcasted_iota(jnp.int32, (TK, 2 * TQ), 0)
    qry_io = lax.broadcasted_iota(jnp.int32, (TK, 2 * TQ), 1) % TQ
    causal = jnp.where(key_io <= qry_io, 0.0, NEG)
    _attend(i, qp_ref, k_ref, vt_ref, bias_ref, o_ref,
            far_masks=lambda j: [mc_ref[g, pl.ds(j, 1), :] for g in range(4)],
            diag_masks=lambda: [causal] * 4)


def _moba_prompt(qf, kab, vat, kmean, bias):
    nb = kab.shape[0]
    t = nb * TK
    return pl.pallas_call(
        _moba_kernel,
        grid=(t // TQ,),
        in_specs=[pl.BlockSpec((TQ, 512), lambda i: (i, 0)),
                  _resident(kab.shape, lambda i: (0, 0, 0)),
                  _resident(vat.shape, lambda i: (0, 0, 0)),
                  _resident(kmean.shape, lambda i: (0, 0)),
                  _resident((8, 2, TK, TQ), lambda i: (0, 0, 0, 0))],
        out_specs=pl.BlockSpec((512, TQ), lambda i: (0, i)),
        out_shape=jax.ShapeDtypeStruct((512, t), F32),
        scratch_shapes=[pltpu.VMEM((4, 256, 2 * TQ), BF16), pltpu.VMEM((4, 128, 2 * TQ), F32)],
        compiler_params=_cparams(1), name="moba_prompt_attention",
    )(qf, kab, vat, kmean, bias)


def _fold8(x):
    return jnp.sum(x.reshape(TK // 8, 8, TQ), axis=0)


def _dsa_kernel(qb_ref, qi_ref, sm_ref, k_ref, vt_ref, ksm_ref, bias_ref, o_ref, qp_ref, sc_ref,
                *, n_sel):
    i = pl.program_id(0)
    qiT = qi_ref[...].T
    zeros = jnp.zeros((64, TQ), F32)
    qi_all = jnp.concatenate(
        [jnp.concatenate([qiT[64 * h:64 * h + 64], zeros], axis=0) for h in range(8)],
        axis=1).astype(BF16)
    wT = sm_ref[...].T
    w_rows = [wT[64 + h:65 + h, :] for h in range(8)]
    key_io = lax.broadcasted_iota(jnp.int32, (TK, TQ), 0)
    qry_io = lax.broadcasted_iota(jnp.int32, (TK, TQ), 1)
    causal = key_io <= qry_io

    def score_tile(j):
        z = _dot(ksm_ref[j], qi_all)
        sc = w_rows[0] * jnp.maximum(z[:, 0:TQ], 0.0)
        for h in range(1, 8):
            sc = sc + w_rows[h] * jnp.maximum(z[:, TQ * h:TQ * (h + 1)], 0.0)
        return sc

    def fold_max(x):
        return jnp.max(x.reshape(TK // 8, 8, TQ), axis=0)

    def fill(j, carry):
        mx, mn = carry
        sc = score_tile(j)
        sc_ref[j] = sc
        return jnp.maximum(mx, fold_max(sc)), jnp.minimum(mn, -fold_max(-sc))

    mx, mn = lax.fori_loop(0, i, fill, (jnp.full((8, TQ), -jnp.inf, F32),
                                         jnp.full((8, TQ), jnp.inf, F32)))
    scd = score_tile(i)
    sc_ref[i] = jnp.where(causal, scd, -jnp.inf)
    mx = jnp.maximum(mx, fold_max(jnp.where(causal, scd, -jnp.inf)))
    mn = jnp.minimum(mn, -fold_max(jnp.where(causal, -scd, -jnp.inf)))
    rmax = jnp.max(mx, axis=0, keepdims=True)
    rmin = jnp.min(mn, axis=0, keepdims=True)

    def count(pred):
        acc = lax.fori_loop(0, i + 1, lambda j, a: a + _fold8(jnp.where(pred(sc_ref[j]), 1.0, 0.0)),
                            jnp.zeros((8, TQ), F32))
        return jnp.sum(acc, axis=0, keepdims=True)

    def count3(j, acc):
        t = sc_ref[j]
        return (acc[0] + _fold8(jnp.where(t >= rmax, 1.0, 0.0)),
                acc[1] + _fold8(jnp.where(t >= 0.0, 1.0, 0.0)),
                acc[2] + _fold8(jnp.where(t > 0.0, 1.0, 0.0)))

    init = lax.fori_loop(0, i + 1, count3, (jnp.zeros((8, TQ), F32),) * 3)
    init = [jnp.sum(a, axis=0, keepdims=True) for a in init]
    n_adm = (i * TQ + 1).astype(F32) + lax.broadcasted_iota(jnp.int32, (1, TQ), 1).astype(F32)
    tau, check = _kth_largest(lambda t: count(lambda x: x >= t), init, rmin, rmax, n_adm, n_sel)

    @pl.when(jnp.sum(check) > 0.0)
    def _():
        need = jnp.float32(n_sel) - count(lambda x: x > tau)
        lower = jnp.where(lax.broadcasted_iota(jnp.int32, (TK, TK), 0)
                          >= lax.broadcasted_iota(jnp.int32, (TK, TK), 1), 1.0, 0.0).astype(BF16)

        def fix(j, seen):
            t = sc_ref[j]
            eq = jnp.where((t == tau) & (check > 0.0), 1.0, 0.0)
            rank = _dot(lower, eq.astype(BF16)) + seen
            sc_ref[j] = jnp.where((eq > 0.0) & (rank > need), -jnp.inf, t)
            return seen + jnp.sum(_fold8(eq), axis=0, keepdims=True)

        lax.fori_loop(0, i + 1, fix, jnp.zeros((1, TQ), F32))

    _store_padded_queries(qp_ref, qb_ref)

    def sel_masks(j):
        mb = jnp.where(sc_ref[j] >= tau, 0.0, NEG)
        return [jnp.concatenate([mb, mb], axis=1)] * 4

    _attend(i, qp_ref, k_ref, vt_ref, bias_ref, o_ref,
            far_masks=sel_masks, diag_masks=lambda: sel_masks(i))


def _dsa_prompt(qf, sm, kbb, vbt, smb, bias, n_sel):
    nb = kbb.shape[0]
    t = nb * TK
    return pl.pallas_call(
        functools.partial(_dsa_kernel, n_sel=n_sel),
        grid=(t // TQ,),
        in_specs=[pl.BlockSpec((TQ, 512), lambda i: (i, 1)),
                  pl.BlockSpec((TQ, 512), lambda i: (i, 2)),
                  pl.BlockSpec((TQ, 128), lambda i: (i, 0)),
                  _resident(kbb.shape, lambda i: (0, 0, 0)),
                  _resident(vbt.shape, lambda i: (0, 0, 0)),
                  _resident(smb.shape, lambda i: (0, 0, 0)),
                  _resident((8, 2, TK, TQ), lambda i: (1, 0, 0, 0))],
        out_specs=pl.BlockSpec((512, TQ), lambda i: (0, i)),
        out_shape=jax.ShapeDtypeStruct((512, t), F32),
        scratch_shapes=[pltpu.VMEM((4, 256, 2 * TQ), BF16), pltpu.VMEM((nb, TK, TQ), F32)],
        compiler_params=_cparams(1), name="dsa_prompt_attention",
    )(qf, qf, sm, kbb, vbt, smb, bias)


def _ffn_kernel(x_ref, oa_ref, ob_ref, gt1_ref, sh2_ref, sc2_ref, gt2_ref, goa_ref, gob_ref, gf_ref,
                wo_ref, wg_ref, wu_ref, wd_ref, y_ref):
    def feat_norm(o, g):
        return o * lax.rsqrt(jnp.mean(o * o, axis=0, keepdims=True) + EPS) * g

    cat = jnp.concatenate([feat_norm(oa_ref[...], goa_ref[...]),
                           feat_norm(ob_ref[...], gob_ref[...])], axis=0)
    att = _dot(cat.T.astype(BF16), wo_ref[...])
    x1 = x_ref[...] + gt1_ref[...] * att
    xn = x1 * lax.rsqrt(jnp.mean(x1 * x1, axis=-1, keepdims=True) + EPS) * gf_ref[...]
    h = (xn * (1.0 + sc2_ref[...]) + sh2_ref[...]).astype(BF16)
    gate = _dot(h, wg_ref[...])
    up = _dot(h, wu_ref[...])
    act = (gate * jax.nn.sigmoid(gate) * up).astype(BF16)
    y_ref[...] = x1 + gt2_ref[...] * _dot(act, wd_ref[...])


def _out_ffn(x, oat, obt, gt1, sh2, sc2, gt2, goa, gob, g_ffn, wo, wg, wu, wd):
    rows, d = x.shape
    mrows = gt1.shape[0]
    mod_map = (lambda i: (i, 0)) if mrows == rows else (lambda i: (0, 0))
    mod_blk = FFN_TILE if mrows == rows else 1
    mod_spec = pl.BlockSpec((mod_blk, d), mod_map)
    col_spec = pl.BlockSpec((512, FFN_TILE), lambda i: (0, i))
    return pl.pallas_call(
        _ffn_kernel,
        grid=(rows // FFN_TILE,),
        in_specs=[pl.BlockSpec((FFN_TILE, d), lambda i: (i, 0)), col_spec, col_spec,
                  mod_spec, mod_spec, mod_spec, mod_spec,
                  pl.BlockSpec((512, 1), lambda i: (0, 0)),
                  pl.BlockSpec((512, 1), lambda i: (0, 0)),
                  pl.BlockSpec((1, d), lambda i: (0, 0)),
                  _resident(wo.shape, lambda i: (0, 0)),
                  _resident(wg.shape, lambda i: (0, 0)),
                  _resident(wu.shape, lambda i: (0, 0)),
                  _resident(wd.shape, lambda i: (0, 0))],
        out_specs=pl.BlockSpec((FFN_TILE, d), lambda i: (i, 0)),
        out_shape=jax.ShapeDtypeStruct((rows, d), F32),
        compiler_params=_cparams(1), name="out_projection_ffn",
    )(x, oat, obt, gt1, sh2, sc2, gt2, goa, gob, g_ffn, wo, wg, wu, wd)


def _page_specs(feats, ppc):
    def spec(p):
        return pl.BlockSpec((None, feats, 128), lambda n, c, pt: (pt[n, c * ppc + p], 0, 0))
    return [spec(p) for p in range(ppc)]


def _head_sum(z):
    out = z[0:8]
    for h in range(1, 8):
        out = out + z[8 * h:8 * h + 8]
    return out


def _sample_scan_kernel(pt_ref, *refs, ppc):
    mk = refs[0:ppc]
    ik = refs[ppc:2 * ppc]
    qi_ref, w_ref, km_ref, sc_ref = refs[2 * ppc:]
    c = pl.program_id(1)
    ppb = MOBA_BLOCK // 128
    bpc = ppc // ppb

    @pl.when(c == 0)
    def _():
        km_ref[...] = jnp.zeros(km_ref.shape, F32)

    lane = lax.broadcasted_iota(jnp.int32, (256, 128), 1)
    km = km_ref[...]
    for b in range(bpc):
        ksum = mk[ppb * b][...]
        for p in range(1, ppb):
            ksum = ksum + mk[ppb * b + p][...]
        kmean = jnp.sum(ksum, axis=1, keepdims=True) * (1.0 / MOBA_BLOCK)
        km = km + jnp.where(lane == c * bpc + b, kmean, 0.0)
    km_ref[...] = km
    qi = qi_ref[...]
    w = w_ref[...]
    for p in range(ppc):
        z = _dot(qi, ik[p][...].astype(BF16))
        sc_ref[:, 128 * p:128 * (p + 1)] = _head_sum(jnp.maximum(z, 0.0) * w)


def _sample_scan(page_table, cache_mk, cache_ik, qi_rows, w_col):
    n, n_pages = page_table.shape
    ppc = PAGES_PER_STEP
    nc = n_pages // ppc
    grid_spec = pltpu.PrefetchScalarGridSpec(
        num_scalar_prefetch=1, grid=(n, nc),
        in_specs=_page_specs(256, ppc) + _page_specs(64, ppc) + [
            pl.BlockSpec((None, 64, 64), lambda n, c, pt: (n, 0, 0)),
            pl.BlockSpec((None, 64, 1), lambda n, c, pt: (n, 0, 0))],
        out_specs=(pl.BlockSpec((None, 256, 128), lambda n, c, pt: (n, 0, 0)),
                   pl.BlockSpec((None, 8, ppc * 128), lambda n, c, pt: (n, 0, c))))
    return pl.pallas_call(
        functools.partial(_sample_scan_kernel, ppc=ppc),
        grid_spec=grid_spec,
        out_shape=(jax.ShapeDtypeStruct((n, 256, 128), F32),
                   jax.ShapeDtypeStruct((n, 8, n_pages * 128), F32)),
        compiler_params=_cparams(2), name="sample_page_scan",
    )(page_table, *([cache_mk] * ppc), *([cache_ik] * ppc), qi_rows, w_col)


def _sample_select_kernel(sc_ref, km_ref, qa_ref, qi_ref, w_ref, kin_ref, bm_ref, mb_ref,
                          *, n_blocks, n_new, n_sel, idx_iter):
    nb = SELECT_BATCH
    rows = nb * 64
    lane = lax.broadcasted_iota(jnp.int32, (rows, 128), 1).astype(F32)
    valid = lane < float(n_blocks)
    gate = jnp.concatenate([_dot(qa_ref[b], km_ref[b].astype(BF16)) for b in range(nb)], axis=0)
    gate = jnp.where(valid, gate, -jnp.inf)
    sel = jnp.zeros_like(gate)
    for _ in range(min(MOBA_TOPK, n_blocks)):
        mx = jnp.max(gate, axis=1, keepdims=True)
        first = jnp.min(jnp.where(gate == mx, lane, 1e9), axis=1, keepdims=True)
        hit = lane == first
        sel = jnp.where(hit & valid, 1.0, sel)
        gate = jnp.where(hit, -jnp.inf, gate)
    bm_ref[...] = jnp.where(sel > 0.0, 0.0, NEG).reshape(nb, 64, 128)

    new = jnp.concatenate(
        [_head_sum(jnp.maximum(_dot(qi_ref[b], kin_ref[b]), 0.0) * w_ref[b]) for b in range(nb)], axis=0)
    tok = lax.broadcasted_iota(jnp.int32, (nb * 8, 128), 0) % 8
    col = lax.broadcasted_iota(jnp.int32, (nb * 8, 128), 1)
    new = jnp.where((col <= tok) & (col < n_new), new, -jnp.inf)
    past = sc_ref.shape[-1]
    a = jnp.concatenate([sc_ref[...].reshape(nb * 8, past), new], axis=1)
    width = past + 128
    idx = lax.broadcasted_iota(jnp.int32, (nb * 8, width), 1).astype(F32)
    adm = a > -jnp.inf

    def count(mask):
        return jnp.sum(jnp.where(mask, 1.0, 0.0), axis=1, keepdims=True)

    rmax = jnp.max(a, axis=1, keepdims=True)
    rmin = jnp.min(jnp.where(adm, a, jnp.inf), axis=1, keepdims=True)
    init = (count(a >= rmax), count(a >= 0.0), count(a > 0.0))
    tau, check = _kth_largest(lambda t: count(a >= t), init, rmin, rmax, count(adm), n_sel)

    eq = a == tau
    need = jnp.float32(n_sel) - count(a > tau)

    def bis(_, carry):
        lj, hj = carry
        mj = jnp.floor((lj + hj) * 0.5)
        ok = count(eq & (idx <= mj)) >= need
        return jnp.where(ok, lj, mj), jnp.where(ok, mj, hj)

    _, hj = lax.fori_loop(0, idx_iter, bis, (jnp.full((nb * 8, 1), -1.0, F32),
                                             jnp.full((nb * 8, 1), float(width - 1), F32)))
    hj = jnp.where(check > 0.0, hj, float(width))
    keep = (a > tau) | (eq & (idx <= hj))
    mb_ref[...] = jnp.where(keep, 0.0, NEG).reshape(nb, 8, width)


def _sample_select(scores, kmean_t, qa_rows, qi_rows, w_col, ki_new_t, n_blocks, n_new, n_sel):
    n, _, past = scores.shape
    nb = SELECT_BATCH
    width = past + 128
    kern = functools.partial(_sample_select_kernel, n_blocks=n_blocks, n_new=n_new, n_sel=n_sel,
                             idx_iter=int(math.ceil(math.log2(width))) + 1)
    per_n = lambda *shape: pl.BlockSpec((nb,) + shape, lambda b: (b,) + (0,) * len(shape))
    return pl.pallas_call(
        kern, grid=(n // nb,),
        in_specs=[per_n(8, past), per_n(256, 128), per_n(64, 256), per_n(64, 64), per_n(64, 1),
                  per_n(64, 128)],
        out_specs=(per_n(64, 128), per_n(8, width)),
        out_shape=(jax.ShapeDtypeStruct((n, 64, 128), F32),
                   jax.ShapeDtypeStruct((n, 8, width), F32)),
        compiler_params=_cparams(1), name="sample_selection",
    )(scores, kmean_t, qa_rows, qi_rows, w_col, ki_new_t)


def _row_update(s, pv, m_ref, l_ref, acc_ref):
    m = m_ref[...]
    m_new = jnp.maximum(m, jnp.max(s, axis=1, keepdims=True))
    alpha = jnp.exp2(m - m_new)
    p = jnp.exp2(s - m_new)
    l_ref[...] = alpha * l_ref[...] + jnp.sum(p, axis=1, keepdims=True)
    acc_ref[...] = alpha * acc_ref[...] + pv(p.astype(BF16))
    m_ref[...] = m_new


def _sample_attn_kernel(pt_ref, *refs, ppc, n_new):
    mk = refs[0:ppc]
    mv = refs[ppc:2 * ppc]
    dk = refs[2 * ppc:3 * ppc]
    dv = refs[3 * ppc:4 * ppc]
    (qa_ref, qb_ref, bm_ref, mbc_ref, mbn_ref, bl_ref, bn_ref, kan_ref, van_ref, kbn_ref, vbn_ref,
     oa_ref, ob_ref, ma_ref, la_ref, acca_ref, mb_ref, lb_ref, accb_ref) = refs[4 * ppc:]
    c = pl.program_id(1)
    last = c == pl.num_programs(1) - 1
    ppb = MOBA_BLOCK // 128
    bpc = ppc // ppb

    @pl.when(c == 0)
    def _():
        for m_ref, l_ref, acc_ref in ((ma_ref, la_ref, acca_ref), (mb_ref, lb_ref, accb_ref)):
            m_ref[...] = jnp.full(m_ref.shape, NEG, F32)
            l_ref[...] = jnp.zeros(l_ref.shape, F32)
            acc_ref[...] = jnp.zeros(acc_ref.shape, F32)

    lane = lax.broadcasted_iota(jnp.int32, (64, 128), 1)
    tile8 = lambda x: jnp.concatenate([x] * 8, axis=0)

    def logits(q_ref, pages):
        q = q_ref[...]
        return jnp.concatenate([_dot(q, pages[p][...].astype(BF16)) for p in range(ppc)], axis=1)

    def values(pages):
        def pv(p):
            out = _dot_nt(p[:, 0:128], pages[0][...].astype(BF16))
            for k in range(1, ppc):
                out = out + _dot_nt(p[:, 128 * k:128 * (k + 1)], pages[k][...].astype(BF16))
            return out
        return pv

    def near(table):
        return jnp.concatenate([jnp.zeros((64, (ppc - 1) * 128), F32), jnp.where(last, table, 0.0)], axis=1)

    bm = bm_ref[...]
    gate_cols = [jnp.sum(jnp.where(lane == c * bpc + b, bm, 0.0), axis=1, keepdims=True)
                 for b in range(bpc)]
    gate = jnp.concatenate([jnp.broadcast_to(col, (64, MOBA_BLOCK)) for col in gate_cols], axis=1)
    _row_update(logits(qa_ref, mk) + gate + near(bl_ref[0]), values(mv), ma_ref, la_ref, acca_ref)
    _row_update(logits(qb_ref, dk) + tile8(mbc_ref[...]) + near(bl_ref[1]), values(dv),
                mb_ref, lb_ref, accb_ref)

    @pl.when(last)
    def _():
        tok = lax.broadcasted_iota(jnp.int32, (64, 128), 0) % 8
        own = (lane <= tok) & (lane < n_new)
        s = jnp.where(own, _dot_nt(qa_ref[...], kan_ref[...]) + bn_ref[0], NEG)
        _row_update(s, lambda p: _dot(p, van_ref[...]), ma_ref, la_ref, acca_ref)
        s = _dot_nt(qb_ref[...], kbn_ref[...]) + bn_ref[1] + tile8(mbn_ref[...])
        _row_update(s, lambda p: _dot(p, vbn_ref[...]), mb_ref, lb_ref, accb_ref)
        grp = lax.broadcasted_iota(jnp.int32, (64, 64), 0) // 16
        for o_ref, l_ref, acc_ref in ((oa_ref, la_ref, acca_ref), (ob_ref, lb_ref, accb_ref)):
            o = acc_ref[...] / l_ref[...]
            out = jnp.zeros((64, 64), F32)
            for g in range(4):
                out = out + jnp.where(grp == g, o[:, 64 * g:64 * g + 64], 0.0)
            o_ref[...] = out


def _sample_attend(page_table, cache_mk, cache_mv, cache_dk, cache_dv, qa_rows, qb_rows, bm, mb,
                   bias_last, bias_new, ka_new, va_new, kb_new, vb_new, n_new):
    n, n_pages = page_table.shape
    ppc = PAGES_PER_STEP
    nc = n_pages // ppc
    per_n = lambda *shape: pl.BlockSpec((None,) + shape, lambda b, c, pt: (b,) + (0,) * len(shape))
    const = lambda *shape: pl.BlockSpec(shape, lambda b, c, pt: (0,) * len(shape))
    in_specs = (_page_specs(256, ppc) * 4 + [
        per_n(64, 256), per_n(64, 256), per_n(64, 128),
        pl.BlockSpec((None, 8, ppc * 128), lambda b, c, pt: (b, 0, c)),
        pl.BlockSpec((None, 8, 128), lambda b, c, pt: (b, 0, n_pages)),
        const(2, 64, 128), const(2, 64, 128),
        per_n(128, 256), per_n(128, 256), per_n(128, 256), per_n(128, 256)])
    grid_spec = pltpu.PrefetchScalarGridSpec(
        num_scalar_prefetch=1, grid=(n, nc), in_specs=in_specs,
        out_specs=(per_n(64, 64), per_n(64, 64)),
        scratch_shapes=[pltpu.VMEM((64, 1), F32), pltpu.VMEM((64, 1), F32), pltpu.VMEM((64, 256), F32),
                        pltpu.VMEM((64, 1), F32), pltpu.VMEM((64, 1), F32), pltpu.VMEM((64, 256), F32)])
    return pl.pallas_call(
        functools.partial(_sample_attn_kernel, ppc=ppc, n_new=n_new),
        grid_spec=grid_spec,
        out_shape=(jax.ShapeDtypeStruct((n, 64, 64), F32), jax.ShapeDtypeStruct((n, 64, 64), F32)),
        compiler_params=_cparams(2), name="sample_paged_attention",
    )(page_table, *([cache_mk] * ppc), *([cache_mv] * ppc), *([cache_dk] * ppc), *([cache_dv] * ppc),
      qa_rows, qb_rows, bm, mb, mb, bias_last, bias_new, ka_new, va_new, kb_new, vb_new)


def _head_token_rows(q, n, s):
    q = q.reshape(n, s, 8, HEAD_DIM).transpose(0, 2, 1, 3)
    q = jnp.pad(q, ((0, 0), (0, 0), (0, 8 - s), (0, 0)))
    return q.reshape(n, 64, HEAD_DIM)


def _block_diag_rows(q_rows):
    n = q_rows.shape[0]
    grp = jnp.arange(64) // 16
    onehot = (grp[:, None] == jnp.arange(4)[None, :]).astype(q_rows.dtype)
    return (q_rows[:, :, None, :] * onehot[None, :, :, None]).reshape(n, 64, 256)


def _pad_new_rows(x, n, s):
    w = x.shape[1]
    return jnp.pad(x.reshape(n, s, w), ((0, 0), (0, 128 - s), (0, 0))).astype(BF16)


def _features_major_pages(cache):
    n_pool, page = cache.shape[:2]
    return jnp.moveaxis(cache.reshape(n_pool, page, -1), 1, 2)


def kernel(x_prompt, x_sample, cache_moba_k, cache_moba_v, cache_dsa_k, cache_dsa_v, cache_idx_k,
           page_table, c_prompt, c_sample, rel_bias, g_attn, g_ffn, w_ada, b_ada, w_in, g_qa, g_ka,
           g_qb, g_kb, g_oa, g_ob, w_out, w_gate, w_up, w_down):
    batch, seq, d = x_prompt.shape
    n_dec, s_dec, _ = x_sample.shape
    depth = w_in.shape[0]
    page = cache_idx_k.shape[2]
    n_pages = page_table.shape[1]
    past = n_pages * page
    assert batch == 1 and d == 1024 and rel_bias.shape == (N_BUCKETS, 16)
    assert seq % ROW_TILE == 0 and (n_dec * s_dec) % ROW_TILE == 0 and s_dec <= 8
    assert page == 128 and past % MOBA_BLOCK == 0 and n_pages % PAGES_PER_STEP == 0
    assert seq // MOBA_BLOCK <= 128 and past // MOBA_BLOCK <= 128 and n_dec % SELECT_BATCH == 0
    rows_s = n_dec * s_dec
    n_sel_p = min(DSA_TOPK, seq // 4)
    n_sel_s = min(DSA_TOPK, (past + s_dec) // 4)
    idx_width = 512
    wi_scale = float(idx_width) ** -0.5

    bias_p = _bias_tiles(rel_bias, _prompt_bias_patterns())
    bias_s = _bias_tiles(rel_bias, _sample_bias_patterns(s_dec, page))
    bias_last = bias_s[:, 0].reshape(2, 64, 128)
    bias_new = bias_s[:, 1].reshape(2, 64, 128)
    bd = jnp.asarray(np.kron(np.eye(8, dtype=np.float32), np.ones((64, 64), np.float32)), BF16)

    xp = x_prompt.reshape(seq, d)
    xs = x_sample.reshape(rows_s, d)
    pad_rows = (-(1 + n_dec)) % 8
    c_all = jnp.concatenate([c_prompt, c_sample, jnp.zeros((pad_rows, d), F32)], axis=0)
    outs_p, outs_s = [], []
    for l in range(depth):
        mod = _modulation(c_all, w_ada[l], b_ada[l][None, :])
        mod_p = [mod[0:1, k * d:(k + 1) * d] for k in range(6)]
        mod_s = [jnp.repeat(mod[1:1 + n_dec, k * d:(k + 1) * d], s_dec, axis=0) for k in range(6)]
        w_main = w_in[l][:, 0:2560].astype(BF16)
        w_small = jnp.pad(w_in[l][:, 2560:2632], ((0, 0), (0, 56))).astype(BF16)
        gq = jnp.stack([jnp.tile(g_qa[l], 8), jnp.tile(g_qb[l], 8)])
        gk = jnp.stack([jnp.tile(g_ka[l], 4), jnp.tile(g_kb[l], 4)])
        goa, gob = g_oa[l][:, None], g_ob[l][:, None]
        wo, wg, wu, wd = (w_out[l].astype(BF16), w_gate[l].astype(BF16), w_up[l].astype(BF16),
                          w_down[l].astype(BF16))

        (qf, ka, va, kb, vb, sm, kab, kbb, vat, vbt, smb, km) = _project(
            xp, mod_p[0], mod_p[1], g_attn[l][None, :], w_main, w_small, bd, gq, gk, wi_scale)
        nb = seq // MOBA_BLOCK
        kmean = jnp.pad(km.reshape(nb, 256), ((0, 128 - nb), (0, 0)))
        oat = _moba_prompt(qf, kab, vat, kmean, bias_p)
        obt = _dsa_prompt(qf, sm, kbb, vbt, smb, bias_p, n_sel_p)
        xp = _out_ffn(xp, oat, obt, mod_p[2], mod_p[3], mod_p[4], mod_p[5], goa, gob,
                      g_ffn[l][None, :], wo, wg, wu, wd)
        outs_p.append((ka.reshape(1, seq, 4, 64), va.reshape(1, seq, 4, 64), kb.reshape(1, seq, 4, 64),
                       vb.reshape(1, seq, 4, 64), sm[:, :64].reshape(1, seq, 64)))

        (qf, ka, va, kb, vb, sm, _, _, _, _, _, _) = _project(
            xs, mod_s[0], mod_s[1], g_attn[l][None, :], w_main, w_small, bd, gq, gk, wi_scale)
        qa_rows = _block_diag_rows(_head_token_rows(qf[:, 0:512], n_dec, s_dec)).astype(BF16)
        qb_rows = _block_diag_rows(_head_token_rows(qf[:, 512:1024], n_dec, s_dec)).astype(BF16)
        qi_rows = _head_token_rows(qf[:, 1024:1536], n_dec, s_dec).astype(BF16)
        w_col = jnp.pad(sm[:, 64:72].reshape(n_dec, s_dec, 8).transpose(0, 2, 1),
                        ((0, 0), (0, 0), (0, 8 - s_dec))).reshape(n_dec, 64, 1)
        cmk = _features_major_pages(cache_moba_k[l])
        cmv = _features_major_pages(cache_moba_v[l])
        cdk = _features_major_pages(cache_dsa_k[l])
        cdv = _features_major_pages(cache_dsa_v[l])
        cik = _features_major_pages(cache_idx_k[l])
        km_s, scores = _sample_scan(page_table, cmk, cik, qi_rows, w_col)
        nbp = past // MOBA_BLOCK
        ki_new_t = jnp.swapaxes(_pad_new_rows(sm[:, :64], n_dec, s_dec), 1, 2)
        bm, mb = _sample_select(scores, km_s, qa_rows, qi_rows, w_col, ki_new_t, nbp, s_dec, n_sel_s)
        oa_s, ob_s = _sample_attend(
            page_table, cmk, cmv, cdk, cdv, qa_rows, qb_rows, bm, mb, bias_last, bias_new,
            _pad_new_rows(ka, n_dec, s_dec), _pad_new_rows(va, n_dec, s_dec),
            _pad_new_rows(kb, n_dec, s_dec), _pad_new_rows(vb, n_dec, s_dec), s_dec)
        to_t = lambda o: o.reshape(n_dec, 8, 8, 64)[:, :, :s_dec].transpose(1, 3, 0, 2).reshape(512, rows_s)
        xs = _out_ffn(xs, to_t(oa_s), to_t(ob_s), mod_s[2], mod_s[3], mod_s[4], mod_s[5], goa, gob,
                      g_ffn[l][None, :], wo, wg, wu, wd)
        outs_s.append((ka.reshape(n_dec, s_dec, 4, 64), va.reshape(n_dec, s_dec, 4, 64),
                       kb.reshape(n_dec, s_dec, 4, 64), vb.reshape(n_dec, s_dec, 4, 64),
                       sm[:, :64].reshape(n_dec, s_dec, 64)))

    stack = lambda outs, k: jnp.stack([o[k] for o in outs])
    return (xp.reshape(batch, seq, d), xs.reshape(n_dec, s_dec, d),
            *[stack(outs_p, k) for k in range(5)], *[stack(outs_s, k) for k in range(5)])
```

```python
import functools
import math

import numpy as np
import jax
import jax.numpy as jnp
from jax import lax
from jax.experimental import pallas as pl
from jax.experimental.pallas import tpu as pltpu

F32 = jnp.float32
BF16 = jnp.bfloat16

HEAD_DIM = 64
MOBA_BLOCK = 256
MOBA_TOPK = 3
DSA_TOPK = 256
N_BUCKETS = 32
MAX_DISTANCE = 128
EPS = 1e-6
LOG2E = math.log2(math.e)
NEG = -1e30
LOWEST = -3.0e38
ROW_TILE = 512
FFN_TILE = 256
TQ = 256
TK = 256
ATTEND_TILES = 4
PAGES_PER_STEP = 16
SELECT_BATCH = 8
MAX_BISECT = 256
VMEM_LIMIT = 60 * 1024 * 1024


def _cparams(n_axes):
    return pltpu.CompilerParams(dimension_semantics=("arbitrary",) * n_axes,
                                vmem_limit_bytes=VMEM_LIMIT)


def _resident(block_shape, index_map):
    return pl.BlockSpec(block_shape, index_map, pipeline_mode=pl.Buffered(1))


def _dot(a, b):
    return jnp.dot(a, b, preferred_element_type=F32)


def _dot_nt(a, b):
    return lax.dot_general(a, b, (((1,), (1,)), ((), ())), preferred_element_type=F32)


def _bucket_np(dist):
    n = np.maximum(dist, 0)
    exact = N_BUCKETS // 2
    ratio = np.log(np.maximum(n, 1).astype(np.float32) / np.float32(exact)) / np.float32(
        math.log(MAX_DISTANCE / exact))
    large = np.minimum(exact + (ratio * np.float32(N_BUCKETS - exact)).astype(np.int32), N_BUCKETS - 1)
    return np.where(n < exact, n, large).astype(np.int32)


def _mod_kernel(c_ref, w_ref, b_ref, o_ref):
    c = c_ref[...]
    s = c * jax.nn.sigmoid(c)
    o_ref[...] = jnp.dot(s, w_ref[...], preferred_element_type=F32,
                         precision=lax.Precision.HIGHEST) + b_ref[...]


def _modulation(c_all, w_ada, b_ada):
    rows, d = c_all.shape
    n6 = w_ada.shape[1]
    tn = 1024
    return pl.pallas_call(
        _mod_kernel,
        grid=(n6 // tn,),
        in_specs=[pl.BlockSpec((rows, d), lambda j: (0, 0)),
                  pl.BlockSpec((d, tn), lambda j: (0, j)),
                  pl.BlockSpec((1, tn), lambda j: (0, j))],
        out_specs=pl.BlockSpec((rows, tn), lambda j: (0, j)),
        out_shape=jax.ShapeDtypeStruct((rows, n6), F32),
        compiler_params=_cparams(1),
        name="adaln_modulation",
    )(c_all, w_ada, b_ada)


def _bias_kernel(rb_ref, pat_ref, o_ref):
    h = pl.program_id(0)
    pat = pat_ref[...]
    base = rb_ref[N_BUCKETS - 1, h]
    out = jnp.zeros(pat.shape, F32)
    for b in range(N_BUCKETS - 1):
        out = jnp.where(pat == b, rb_ref[b, h] - base, out)
    o_ref[0] = out * LOG2E


def _bias_tiles(rel_bias, pattern):
    n_heads = rel_bias.shape[1]
    nd = pattern.ndim
    zeros = (0,) * nd
    return pl.pallas_call(
        _bias_kernel,
        grid=(n_heads,),
        in_specs=[pl.BlockSpec(memory_space=pltpu.SMEM),
                  pl.BlockSpec(pattern.shape, lambda h: zeros)],
        out_specs=pl.BlockSpec((1,) + pattern.shape, lambda h: (h,) + zeros),
        out_shape=jax.ShapeDtypeStruct((n_heads,) + pattern.shape, F32),
        compiler_params=_cparams(1),
        name="rel_bias_tiles",
    )(rel_bias, jnp.asarray(pattern))


def _prompt_bias_patterns():
    key = np.arange(TK)[:, None]
    qry = np.arange(TQ)[None, :]
    diag = np.where(qry >= key, _bucket_np(qry - key), N_BUCKETS - 1)
    adj = _bucket_np(TK + qry - key)
    return np.stack([diag, adj]).astype(np.int32)


def _sample_bias_patterns(n_new, page):
    s = np.arange(8)[:, None]
    c = np.arange(page)[None, :]
    last = _bucket_np(page + s - c)
    new = np.where((c <= s) & (c < n_new), _bucket_np(s - c), N_BUCKETS - 1)
    return np.stack([last, new]).astype(np.int32)


def _proj_kernel(x_ref, sh_ref, sc_ref, g_ref, wm_ref, ws_ref, bd_ref, gq_ref, gk_ref,
                 qf_ref, ka_ref, va_ref, kb_ref, vb_ref, sm_ref,
                 kab_ref, kbb_ref, vat_ref, vbt_ref, smb_ref, km_ref, *, wi_scale):
    x = x_ref[...]
    xn = x * lax.rsqrt(jnp.mean(x * x, axis=-1, keepdims=True) + EPS) * g_ref[...]
    h = (xn * (1.0 + sc_ref[...]) + sh_ref[...]).astype(BF16)
    y = _dot(h, wm_ref[...])
    ys = _dot(h, ws_ref[...])

    def head_norm(t, g, width):
        ss = _dot((t * t).astype(BF16), bd_ref[:width, :width])
        return t * lax.rsqrt(ss * (1.0 / HEAD_DIM) + EPS) * g

    scale = HEAD_DIM ** -0.5 * LOG2E
    qa = head_norm(y[:, 0:512], gq_ref[0:1, :], 512)
    ka = head_norm(y[:, 512:768], gk_ref[0:1, :], 256)
    va = y[:, 768:1024]
    qb = head_norm(y[:, 1024:1536], gq_ref[1:2, :], 512)
    kb = head_norm(y[:, 1536:1792], gk_ref[1:2, :], 256)
    vb = y[:, 1792:2048]
    qf_ref[:, 0:512] = qa * scale
    qf_ref[:, 512:1024] = qb * scale
    qf_ref[:, 1024:1536] = y[:, 2048:2560]
    ka_ref[...] = ka
    va_ref[...] = va
    kb_ref[...] = kb
    vb_ref[...] = vb
    lane = lax.broadcasted_iota(jnp.int32, (1, 128), 1)
    sm = ys * jnp.where((lane >= 64) & (lane < 72), wi_scale, 1.0)
    sm_ref[...] = sm
    means = []
    for b in range(ROW_TILE // TK):
        rows = slice(b * TK, (b + 1) * TK)
        kab_ref[b] = ka[rows].astype(BF16)
        kbb_ref[b] = kb[rows].astype(BF16)
        vat_ref[b] = va[rows].T.astype(BF16)
        vbt_ref[b] = vb[rows].T.astype(BF16)
        smb_ref[b] = sm[rows].astype(BF16)
        means.append(jnp.mean(ka[rows], axis=0, keepdims=True))
    km_ref[...] = jnp.concatenate(means, axis=0)


def _project(x, shift, scale, g_attn, w_main, w_small, bd, gq, gk, wi_scale):
    rows, d = x.shape
    nt = rows // ROW_TILE
    nb = rows // TK
    bpt = ROW_TILE // TK
    mrows = shift.shape[0]
    mod_map = (lambda i: (i, 0)) if mrows == rows else (lambda i: (0, 0))
    mod_blk = ROW_TILE if mrows == rows else 1
    row_spec = lambda w: pl.BlockSpec((ROW_TILE, w), lambda i: (i, 0))
    blk_spec = lambda w: pl.BlockSpec((bpt, TK, w), lambda i: (i, 0, 0))
    out_shape = (
        jax.ShapeDtypeStruct((rows, 1536), F32),
        jax.ShapeDtypeStruct((rows, 256), F32),
        jax.ShapeDtypeStruct((rows, 256), F32),
        jax.ShapeDtypeStruct((rows, 256), F32),
        jax.ShapeDtypeStruct((rows, 256), F32),
        jax.ShapeDtypeStruct((rows, 128), F32),
        jax.ShapeDtypeStruct((nb, TK, 256), BF16),
        jax.ShapeDtypeStruct((nb, TK, 256), BF16),
        jax.ShapeDtypeStruct((nb, 256, TK), BF16),
        jax.ShapeDtypeStruct((nb, 256, TK), BF16),
        jax.ShapeDtypeStruct((nb, TK, 128), BF16),
        jax.ShapeDtypeStruct((nt, bpt, 256), F32),
    )
    out_specs = (row_spec(1536), row_spec(256), row_spec(256), row_spec(256), row_spec(256),
                 row_spec(128), blk_spec(256), blk_spec(256), blk_spec(TK), blk_spec(TK),
                 blk_spec(128), pl.BlockSpec((None, bpt, 256), lambda i: (i, 0, 0)))
    in_specs = [row_spec(d),
                pl.BlockSpec((mod_blk, d), mod_map),
                pl.BlockSpec((mod_blk, d), mod_map),
                pl.BlockSpec((1, d), lambda i: (0, 0)),
                _resident(w_main.shape, lambda i: (0, 0)),
                _resident(w_small.shape, lambda i: (0, 0)),
                _resident(bd.shape, lambda i: (0, 0)),
                pl.BlockSpec(gq.shape, lambda i: (0, 0)),
                pl.BlockSpec(gk.shape, lambda i: (0, 0))]
    return pl.pallas_call(
        functools.partial(_proj_kernel, wi_scale=wi_scale),
        grid=(nt,), in_specs=in_specs, out_specs=out_specs, out_shape=out_shape,
        compiler_params=_cparams(1), name="adaln_in_projection",
    )(x, shift, scale, g_attn, w_main, w_small, bd, gq, gk)


def _store_padded_queries(qp_ref, q_ref):
    qT = q_ref[...].T
    zeros = jnp.zeros((64, 2 * TQ), BF16)
    for g in range(4):
        q2 = jnp.concatenate([qT[128 * g:128 * g + 64], qT[128 * g + 64:128 * g + 128]],
                             axis=1).astype(BF16)
        qp_ref[g] = jnp.concatenate([q2 if gg == g else zeros for gg in range(4)], axis=0)


def _for_tile_pairs(n, body, carry, per=2):
    start = 0
    while per >= 1:
        trips = (n - start) // per
        carry = lax.fori_loop(
            0, trips, lambda t, c, s=start, p=per: body([s + p * t + k for k in range(p)], c), carry)
        start = start + per * trips
        per //= 2
    return carry


def _attend(i, qp_ref, k_ref, vt_ref, bias_ref, o_ref, far_masks, diag_masks):
    def tile_update(tiles, carry, tile_masks, near):
        kt = jnp.concatenate([k_ref[j] for j in tiles], axis=0)
        logits = [_dot(kt, qp_ref[g]) for g in range(4)]
        out = []
        for g in range(4):
            m, l, acc = carry[3 * g:3 * g + 3]
            mask = jnp.concatenate([jnp.broadcast_to(tm[g], (TK, 2 * TQ)) for tm in tile_masks], axis=0)
            s = logits[g] + mask
            if near is not None:
                s = s + jnp.concatenate([bias_ref[2 * g, near], bias_ref[2 * g + 1, near]], axis=1)
            m_new = jnp.maximum(m, jnp.max(s, axis=0, keepdims=True))
            if g < 3:
                m_new = jnp.maximum(m_new, logits[g + 1][0:1, :] * 0.0 + NEG)
            alpha = jnp.exp2(m - m_new)
            p = jnp.exp2(s - m_new)
            l = alpha * l + jnp.sum(p, axis=0, keepdims=True)
            vt = jnp.concatenate([vt_ref[j, 64 * g:64 * g + 64, :] for j in tiles], axis=1)
            acc = alpha * acc + _dot(vt, p.astype(BF16))
            out.extend((m_new, l, acc))
        return tuple(out)

    carry = (jnp.full((1, 2 * TQ), NEG, F32), jnp.zeros((1, 2 * TQ), F32),
             jnp.zeros((HEAD_DIM, 2 * TQ), F32)) * 4
    n_far = jnp.maximum(i - 1, 0)
    carry = _for_tile_pairs(n_far, lambda tiles, c: tile_update(tiles, c, [far_masks(j) for j in tiles], None),
                            carry, per=ATTEND_TILES)
    carry = lax.fori_loop(n_far, i, lambda j, c: tile_update([j], c, [far_masks(j)], 1), carry)
    carry = tile_update([i], carry, [diag_masks()], 0)
    for g in range(4):
        out = carry[3 * g + 2] / carry[3 * g + 1]
        o_ref[128 * g:128 * g + 64, :] = out[:, :TQ]
        o_ref[128 * g + 64:128 * g + 128, :] = out[:, TQ:]


def _kth_largest(count_ge, counts_init, rmin, rmax, n_adm, n_sel):
    ksel = jnp.float32(n_sel)
    c_top, c_zero, c_pos = counts_init
    few = n_adm <= ksel
    top = ~few & (c_top >= ksel)
    zero = ~few & ~top & (c_pos < ksel) & (c_zero >= ksel)
    pos = c_pos >= ksel
    active = jnp.where(few | top | zero, 0.0, 1.0)
    tau = jnp.where(few, LOWEST, jnp.where(top, rmax, 0.0))
    check = jnp.where(top | zero, 1.0, 0.0)
    lo = jnp.where(pos, 0.0, rmin)
    hi = jnp.where(pos, rmax, 0.0)

    def cond(st):
        return (st[0] < MAX_BISECT) & (jnp.sum(st[1]) > 0.0)

    def step(st):
        it, active, lo, hi, tau, check = st
        mid = lo + (hi - lo) * 0.5
        stuck = (mid <= lo) | (mid >= hi)
        c = count_ge(mid)
        act = active > 0.0
        found = act & ~stuck & (c == ksel)
        ended = act & stuck
        ge = c >= ksel
        tau = jnp.where(found, mid, jnp.where(ended, lo, tau))
        check = jnp.where(ended, 1.0, check)
        lo = jnp.where(ge, mid, lo)
        hi = jnp.where(ge, hi, mid)
        active = jnp.where(found | ended, 0.0, active)
        return it + 1, active, lo, hi, tau, check

    _, active, lo, _, tau, check = lax.while_loop(
        cond, step, (jnp.int32(0), active, lo, hi, tau, check))
    left = active > 0.0
    return jnp.where(left, lo, tau), jnp.where(left, 1.0, check)


def _moba_kernel(q_ref, k_ref, vt_ref, km_ref, bias_ref, o_ref, qp_ref, mc_ref):
    i = pl.program_id(0)
    _store_padded_queries(qp_ref, q_ref)
    blk = lax.broadcasted_iota(jnp.int32, (128, 2 * TQ), 0).astype(F32)
    valid = blk < i.astype(F32)
    kmb = km_ref[...].astype(BF16)
    for g in range(4):
        gate = jnp.where(valid, _dot(kmb, qp_ref[g]), -jnp.inf)
        sel = jnp.zeros_like(gate)
        for _ in range(MOBA_TOPK):
            mx = jnp.max(gate, axis=0, keepdims=True)
            first = jnp.min(jnp.where(gate == mx, blk, 1e9), axis=0, keepdims=True)
            hit = blk == first
            sel = jnp.where(hit & valid, 1.0, sel)
            gate = jnp.where(hit, -jnp.inf, gate)
        mc_ref[g] = jnp.where(sel > 0.0, 0.0, NEG)
    key_io = lax.broadcasted_iota(jnp.int32, (TK, 2 * TQ), 0)
    qry_io = lax.broadcasted_iota(jnp.int32, (TK, 2 * TQ), 1) % TQ
    causal = jnp.where(key_io <= qry_io, 0.0, NEG)
    _attend(i, qp_ref, k_ref, vt_ref, bias_ref, o_ref,
            far_masks=lambda j: [mc_ref[g, pl.ds(j, 1), :] for g in range(4)],
            diag_masks=lambda: [causal] * 4)


def _moba_prompt(qf, kab, vat, kmean, bias):
    nb = kab.shape[0]
    t = nb * TK
    return pl.pallas_call(
        _moba_kernel,
        grid=(t // TQ,),
        in_specs=[pl.BlockSpec((TQ, 512), lambda i: (i, 0)),
                  _resident(kab.shape, lambda i: (0, 0, 0)),
                  _resident(vat.shape, lambda i: (0, 0, 0)),
                  _resident(kmean.shape, lambda i: (0, 0)),
                  _resident((8, 2, TK, TQ), lambda i: (0, 0, 0, 0))],
        out_specs=pl.BlockSpec((512, TQ), lambda i: (0, i)),
        out_shape=jax.ShapeDtypeStruct((512, t), F32),
        scratch_shapes=[pltpu.VMEM((4, 256, 2 * TQ), BF16), pltpu.VMEM((4, 128, 2 * TQ), F32)],
        compiler_params=_cparams(1), name="moba_prompt_attention",
    )(qf, kab, vat, kmean, bias)


def _fold8(x):
    return jnp.sum(x.reshape(TK // 8, 8, TQ), axis=0)


def _dsa_kernel(qb_ref, qi_ref, sm_ref, k_ref, vt_ref, ksm_ref, bias_ref, o_ref, qp_ref, sc_ref,
                *, n_sel):
    i = pl.program_id(0)
    qiT = qi_ref[...].T
    zeros = jnp.zeros((64, TQ), F32)
    qi_all = jnp.concatenate(
        [jnp.concatenate([qiT[64 * h:64 * h + 64], zeros], axis=0) for h in range(8)],
        axis=1).astype(BF16)
    wT = sm_ref[...].T
    w_rows = [wT[64 + h:65 + h, :] for h in range(8)]
    key_io = lax.broadcasted_iota(jnp.int32, (TK, TQ), 0)
    qry_io = lax.broadcasted_iota(jnp.int32, (TK, TQ), 1)
    causal = key_io <= qry_io

    def score_tile(j):
        z = _dot(ksm_ref[j], qi_all)
        sc = w_rows[0] * jnp.maximum(z[:, 0:TQ], 0.0)
        for h in range(1, 8):
            sc = sc + w_rows[h] * jnp.maximum(z[:, TQ * h:TQ * (h + 1)], 0.0)
        return sc

    def fold_max(x):
        return jnp.max(x.reshape(TK // 8, 8, TQ), axis=0)

    def fill(tiles, carry):
        mx, mn = carry
        for j in tiles:
            sc = score_tile(j)
            sc_ref[j] = sc
            mx, mn = jnp.maximum(mx, fold_max(sc)), jnp.minimum(mn, -fold_max(-sc))
        return mx, mn

    mx, mn = _for_tile_pairs(i, fill, (jnp.full((8, TQ), -jnp.inf, F32),
                                       jnp.full((8, TQ), jnp.inf, F32)))
    scd = score_tile(i)
    sc_ref[i] = jnp.where(causal, scd, -jnp.inf)
    mx = jnp.maximum(mx, fold_max(jnp.where(causal, scd, -jnp.inf)))
    mn = jnp.minimum(mn, -fold_max(jnp.where(causal, -scd, -jnp.inf)))
    rmax = jnp.max(mx, axis=0, keepdims=True)
    rmin = jnp.min(mn, axis=0, keepdims=True)

    def count(pred):
        def body(tiles, acc):
            for j in tiles:
                acc = acc + _fold8(jnp.where(pred(sc_ref[j]), 1.0, 0.0))
            return acc
        return jnp.sum(_for_tile_pairs(i + 1, body, jnp.zeros((8, TQ), F32)), axis=0, keepdims=True)

    def count3(j, acc):
        t = sc_ref[j]
        return (acc[0] + _fold8(jnp.where(t >= rmax, 1.0, 0.0)),
                acc[1] + _fold8(jnp.where(t >= 0.0, 1.0, 0.0)),
                acc[2] + _fold8(jnp.where(t > 0.0, 1.0, 0.0)))

    init = lax.fori_loop(0, i + 1, count3, (jnp.zeros((8, TQ), F32),) * 3)
    init = [jnp.sum(a, axis=0, keepdims=True) for a in init]
    n_adm = (i * TQ + 1).astype(F32) + lax.broadcasted_iota(jnp.int32, (1, TQ), 1).astype(F32)
    tau, check = _kth_largest(lambda t: count(lambda x: x >= t), init, rmin, rmax, n_adm, n_sel)

    @pl.when(jnp.sum(check) > 0.0)
    def _():
        need = jnp.float32(n_sel) - count(lambda x: x > tau)
        lower = jnp.where(lax.broadcasted_iota(jnp.int32, (TK, TK), 0)
                          >= lax.broadcasted_iota(jnp.int32, (TK, TK), 1), 1.0, 0.0).astype(BF16)

        def fix(j, seen):
            t = sc_ref[j]
            eq = jnp.where((t == tau) & (check > 0.0), 1.0, 0.0)
            rank = _dot(lower, eq.astype(BF16)) + seen
            sc_ref[j] = jnp.where((eq > 0.0) & (rank > need), -jnp.inf, t)
            return seen + jnp.sum(_fold8(eq), axis=0, keepdims=True)

        lax.fori_loop(0, i + 1, fix, jnp.zeros((1, TQ), F32))

    _store_padded_queries(qp_ref, qb_ref)

    def sel_masks(j):
        mb = jnp.where(sc_ref[j] >= tau, 0.0, NEG)
        return [jnp.concatenate([mb, mb], axis=1)] * 4

    _attend(i, qp_ref, k_ref, vt_ref, bias_ref, o_ref,
            far_masks=sel_masks, diag_masks=lambda: sel_masks(i))


def _dsa_prompt(qf, sm, kbb, vbt, smb, bias, n_sel):
    nb = kbb.shape[0]
    t = nb * TK
    return pl.pallas_call(
        functools.partial(_dsa_kernel, n_sel=n_sel),
        grid=(t // TQ,),
        in_specs=[pl.BlockSpec((TQ, 512), lambda i: (i, 1)),
                  pl.BlockSpec((TQ, 512), lambda i: (i, 2)),
                  pl.BlockSpec((TQ, 128), lambda i: (i, 0)),
                  _resident(kbb.shape, lambda i: (0, 0, 0)),
                  _resident(vbt.shape, lambda i: (0, 0, 0)),
                  _resident(smb.shape, lambda i: (0, 0, 0)),
                  _resident((8, 2, TK, TQ), lambda i: (1, 0, 0, 0))],
        out_specs=pl.BlockSpec((512, TQ), lambda i: (0, i)),
        out_shape=jax.ShapeDtypeStruct((512, t), F32),
        scratch_shapes=[pltpu.VMEM((4, 256, 2 * TQ), BF16), pltpu.VMEM((nb, TK, TQ), F32)],
        compiler_params=_cparams(1), name="dsa_prompt_attention",
    )(qf, qf, sm, kbb, vbt, smb, bias)


def _ffn_kernel(x_ref, oa_ref, ob_ref, gt1_ref, sh2_ref, sc2_ref, gt2_ref, goa_ref, gob_ref, gf_ref,
                wo_ref, wg_ref, wu_ref, wd_ref, y_ref):
    def feat_norm(o, g):
        return o * lax.rsqrt(jnp.mean(o * o, axis=0, keepdims=True) + EPS) * g

    cat = jnp.concatenate([feat_norm(oa_ref[...], goa_ref[...]),
                           feat_norm(ob_ref[...], gob_ref[...])], axis=0)
    att = _dot(cat.T.astype(BF16), wo_ref[...])
    x1 = x_ref[...] + gt1_ref[...] * att
    xn = x1 * lax.rsqrt(jnp.mean(x1 * x1, axis=-1, keepdims=True) + EPS) * gf_ref[...]
    h = (xn * (1.0 + sc2_ref[...]) + sh2_ref[...]).astype(BF16)
    gate = _dot(h, wg_ref[...])
    up = _dot(h, wu_ref[...])
    act = (gate * jax.nn.sigmoid(gate) * up).astype(BF16)
    y_ref[...] = x1 + gt2_ref[...] * _dot(act, wd_ref[...])


def _out_ffn(x, oat, obt, gt1, sh2, sc2, gt2, goa, gob, g_ffn, wo, wg, wu, wd):
    rows, d = x.shape
    mrows = gt1.shape[0]
    mod_map = (lambda i: (i, 0)) if mrows == rows else (lambda i: (0, 0))
    mod_blk = FFN_TILE if mrows == rows else 1
    mod_spec = pl.BlockSpec((mod_blk, d), mod_map)
    col_spec = pl.BlockSpec((512, FFN_TILE), lambda i: (0, i))
    return pl.pallas_call(
        _ffn_kernel,
        grid=(rows // FFN_TILE,),
        in_specs=[pl.BlockSpec((FFN_TILE, d), lambda i: (i, 0)), col_spec, col_spec,
                  mod_spec, mod_spec, mod_spec, mod_spec,
                  pl.BlockSpec((512, 1), lambda i: (0, 0)),
                  pl.BlockSpec((512, 1), lambda i: (0, 0)),
                  pl.BlockSpec((1, d), lambda i: (0, 0)),
                  _resident(wo.shape, lambda i: (0, 0)),
                  _resident(wg.shape, lambda i: (0, 0)),
                  _resident(wu.shape, lambda i: (0, 0)),
                  _resident(wd.shape, lambda i: (0, 0))],
        out_specs=pl.BlockSpec((FFN_TILE, d), lambda i: (i, 0)),
        out_shape=jax.ShapeDtypeStruct((rows, d), F32),
        compiler_params=_cparams(1), name="out_projection_ffn",
    )(x, oat, obt, gt1, sh2, sc2, gt2, goa, gob, g_ffn, wo, wg, wu, wd)


def _page_specs(feats, ppc):
    def spec(p):
        return pl.BlockSpec((None, feats, 128), lambda n, c, pt: (pt[n, c * ppc + p], 0, 0))
    return [spec(p) for p in range(ppc)]


def _head_sum(z):
    out = z[0:8]
    for h in range(1, 8):
        out = out + z[8 * h:8 * h + 8]
    return out


def _sample_scan_kernel(pt_ref, *refs, ppc):
    mk = refs[0:ppc]
    ik = refs[ppc:2 * ppc]
    qi_ref, w_ref, km_ref, sc_ref = refs[2 * ppc:]
    c = pl.program_id(1)
    ppb = MOBA_BLOCK // 128
    bpc = ppc // ppb

    @pl.when(c == 0)
    def _():
        km_ref[...] = jnp.zeros(km_ref.shape, F32)

    lane = lax.broadcasted_iota(jnp.int32, (256, 128), 1)
    km = km_ref[...]
    for b in range(bpc):
        ksum = mk[ppb * b][...]
        for p in range(1, ppb):
            ksum = ksum + mk[ppb * b + p][...]
        kmean = jnp.sum(ksum, axis=1, keepdims=True) * (1.0 / MOBA_BLOCK)
        km = km + jnp.where(lane == c * bpc + b, kmean, 0.0)
    km_ref[...] = km
    qi = qi_ref[...]
    w = w_ref[...]
    for p in range(ppc):
        z = _dot(qi, ik[p][...].astype(BF16))
        sc_ref[:, 128 * p:128 * (p + 1)] = _head_sum(jnp.maximum(z, 0.0) * w)


def _sample_scan(page_table, cache_mk, cache_ik, qi_rows, w_col):
    n, n_pages = page_table.shape
    ppc = PAGES_PER_STEP
    nc = n_pages // ppc
    grid_spec = pltpu.PrefetchScalarGridSpec(
        num_scalar_prefetch=1, grid=(n, nc),
        in_specs=_page_specs(256, ppc) + _page_specs(64, ppc) + [
            pl.BlockSpec((None, 64, 64), lambda n, c, pt: (n, 0, 0)),
            pl.BlockSpec((None, 64, 1), lambda n, c, pt: (n, 0, 0))],
        out_specs=(pl.BlockSpec((None, 256, 128), lambda n, c, pt: (n, 0, 0)),
                   pl.BlockSpec((None, 8, ppc * 128), lambda n, c, pt: (n, 0, c))))
    return pl.pallas_call(
        functools.partial(_sample_scan_kernel, ppc=ppc),
        grid_spec=grid_spec,
        out_shape=(jax.ShapeDtypeStruct((n, 256, 128), F32),
                   jax.ShapeDtypeStruct((n, 8, n_pages * 128), F32)),
        compiler_params=_cparams(2), name="sample_page_scan",
    )(page_table, *([cache_mk] * ppc), *([cache_ik] * ppc), qi_rows, w_col)


def _sample_select_kernel(sc_ref, km_ref, qa_ref, qi_ref, w_ref, kin_ref, bm_ref, mb_ref,
                          *, n_blocks, n_new, n_sel, idx_iter):
    nb = SELECT_BATCH
    rows = nb * 64
    lane = lax.broadcasted_iota(jnp.int32, (rows, 128), 1).astype(F32)
    valid = lane < float(n_blocks)
    gate = jnp.concatenate([_dot(qa_ref[b], km_ref[b].astype(BF16)) for b in range(nb)], axis=0)
    gate = jnp.where(valid, gate, -jnp.inf)
    sel = jnp.zeros_like(gate)
    for _ in range(min(MOBA_TOPK, n_blocks)):
        mx = jnp.max(gate, axis=1, keepdims=True)
        first = jnp.min(jnp.where(gate == mx, lane, 1e9), axis=1, keepdims=True)
        hit = lane == first
        sel = jnp.where(hit & valid, 1.0, sel)
        gate = jnp.where(hit, -jnp.inf, gate)
    bm_ref[...] = jnp.where(sel > 0.0, 0.0, NEG).reshape(nb, 64, 128)

    new = jnp.concatenate(
        [_head_sum(jnp.maximum(_dot(qi_ref[b], kin_ref[b]), 0.0) * w_ref[b]) for b in range(nb)], axis=0)
    tok = lax.broadcasted_iota(jnp.int32, (nb * 8, 128), 0) % 8
    col = lax.broadcasted_iota(jnp.int32, (nb * 8, 128), 1)
    new = jnp.where((col <= tok) & (col < n_new), new, -jnp.inf)
    past = sc_ref.shape[-1]
    a = jnp.concatenate([sc_ref[...].reshape(nb * 8, past), new], axis=1)
    width = past + 128
    idx = lax.broadcasted_iota(jnp.int32, (nb * 8, width), 1).astype(F32)
    adm = a > -jnp.inf

    def count(mask):
        return jnp.sum(jnp.where(mask, 1.0, 0.0), axis=1, keepdims=True)

    rmax = jnp.max(a, axis=1, keepdims=True)
    rmin = jnp.min(jnp.where(adm, a, jnp.inf), axis=1, keepdims=True)
    init = (count(a >= rmax), count(a >= 0.0), count(a > 0.0))
    tau, check = _kth_largest(lambda t: count(a >= t), init, rmin, rmax, count(adm), n_sel)

    eq = a == tau
    need = jnp.float32(n_sel) - count(a > tau)

    def bis(_, carry):
        lj, hj = carry
        mj = jnp.floor((lj + hj) * 0.5)
        ok = count(eq & (idx <= mj)) >= need
        return jnp.where(ok, lj, mj), jnp.where(ok, mj, hj)

    _, hj = lax.fori_loop(0, idx_iter, bis, (jnp.full((nb * 8, 1), -1.0, F32),
                                             jnp.full((nb * 8, 1), float(width - 1), F32)))
    hj = jnp.where(check > 0.0, hj, float(width))
    keep = (a > tau) | (eq & (idx <= hj))
    mb_ref[...] = jnp.where(keep, 0.0, NEG).reshape(nb, 8, width)


def _sample_select(scores, kmean_t, qa_rows, qi_rows, w_col, ki_new_t, n_blocks, n_new, n_sel):
    n, _, past = scores.shape
    nb = SELECT_BATCH
    width = past + 128
    kern = functools.partial(_sample_select_kernel, n_blocks=n_blocks, n_new=n_new, n_sel=n_sel,
                             idx_iter=int(math.ceil(math.log2(width))) + 1)
    per_n = lambda *shape: pl.BlockSpec((nb,) + shape, lambda b: (b,) + (0,) * len(shape))
    return pl.pallas_call(
        kern, grid=(n // nb,),
        in_specs=[per_n(8, past), per_n(256, 128), per_n(64, 256), per_n(64, 64), per_n(64, 1),
                  per_n(64, 128)],
        out_specs=(per_n(64, 128), per_n(8, width)),
        out_shape=(jax.ShapeDtypeStruct((n, 64, 128), F32),
                   jax.ShapeDtypeStruct((n, 8, width), F32)),
        compiler_params=_cparams(1), name="sample_selection",
    )(scores, kmean_t, qa_rows, qi_rows, w_col, ki_new_t)


def _row_update(s, pv, m_ref, l_ref, acc_ref):
    m = m_ref[...]
    m_new = jnp.maximum(m, jnp.max(s, axis=1, keepdims=True))
    alpha = jnp.exp2(m - m_new)
    p = jnp.exp2(s - m_new)
    l_ref[...] = alpha * l_ref[...] + jnp.sum(p, axis=1, keepdims=True)
    acc_ref[...] = alpha * acc_ref[...] + pv(p.astype(BF16))
    m_ref[...] = m_new


def _sample_attn_kernel(pt_ref, *refs, ppc, n_new):
    mk = refs[0:ppc]
    mv = refs[ppc:2 * ppc]
    dk = refs[2 * ppc:3 * ppc]
    dv = refs[3 * ppc:4 * ppc]
    (qa_ref, qb_ref, bm_ref, mbc_ref, mbn_ref, bl_ref, bn_ref, kan_ref, van_ref, kbn_ref, vbn_ref,
     oa_ref, ob_ref, ma_ref, la_ref, acca_ref, mb_ref, lb_ref, accb_ref) = refs[4 * ppc:]
    c = pl.program_id(1)
    last = c == pl.num_programs(1) - 1
    ppb = MOBA_BLOCK // 128
    bpc = ppc // ppb

    @pl.when(c == 0)
    def _():
        for m_ref, l_ref, acc_ref in ((ma_ref, la_ref, acca_ref), (mb_ref, lb_ref, accb_ref)):
            m_ref[...] = jnp.full(m_ref.shape, NEG, F32)
            l_ref[...] = jnp.zeros(l_ref.shape, F32)
            acc_ref[...] = jnp.zeros(acc_ref.shape, F32)

    lane = lax.broadcasted_iota(jnp.int32, (64, 128), 1)
    tile8 = lambda x: jnp.concatenate([x] * 8, axis=0)

    def logits(q_ref, pages):
        q = q_ref[...]
        return jnp.concatenate([_dot(q, pages[p][...].astype(BF16)) for p in range(ppc)], axis=1)

    def values(pages):
        def pv(p):
            out = _dot_nt(p[:, 0:128], pages[0][...].astype(BF16))
            for k in range(1, ppc):
                out = out + _dot_nt(p[:, 128 * k:128 * (k + 1)], pages[k][...].astype(BF16))
            return out
        return pv

    def near(table):
        return jnp.concatenate([jnp.zeros((64, (ppc - 1) * 128), F32), jnp.where(last, table, 0.0)], axis=1)

    bm = bm_ref[...]
    gate_cols = [jnp.sum(jnp.where(lane == c * bpc + b, bm, 0.0), axis=1, keepdims=True)
                 for b in range(bpc)]
    gate = jnp.concatenate([jnp.broadcast_to(col, (64, MOBA_BLOCK)) for col in gate_cols], axis=1)
    _row_update(logits(qa_ref, mk) + gate + near(bl_ref[0]), values(mv), ma_ref, la_ref, acca_ref)
    _row_update(logits(qb_ref, dk) + tile8(mbc_ref[...]) + near(bl_ref[1]), values(dv),
                mb_ref, lb_ref, accb_ref)

    @pl.when(last)
    def _():
        tok = lax.broadcasted_iota(jnp.int32, (64, 128), 0) % 8
        own = (lane <= tok) & (lane < n_new)
        s = jnp.where(own, _dot_nt(qa_ref[...], kan_ref[...]) + bn_ref[0], NEG)
        _row_update(s, lambda p: _dot(p, van_ref[...]), ma_ref, la_ref, acca_ref)
        s = _dot_nt(qb_ref[...], kbn_ref[...]) + bn_ref[1] + tile8(mbn_ref[...])
        _row_update(s, lambda p: _dot(p, vbn_ref[...]), mb_ref, lb_ref, accb_ref)
        grp = lax.broadcasted_iota(jnp.int32, (64, 64), 0) // 16
        for o_ref, l_ref, acc_ref in ((oa_ref, la_ref, acca_ref), (ob_ref, lb_ref, accb_ref)):
            o = acc_ref[...] / l_ref[...]
            out = jnp.zeros((64, 64), F32)
            for g in range(4):
                out = out + jnp.where(grp == g, o[:, 64 * g:64 * g + 64], 0.0)
            o_ref[...] = out


def _sample_attend(page_table, cache_mk, cache_mv, cache_dk, cache_dv, qa_rows, qb_rows, bm, mb,
                   bias_last, bias_new, ka_new, va_new, kb_new, vb_new, n_new):
    n, n_pages = page_table.shape
    ppc = PAGES_PER_STEP
    nc = n_pages // ppc
    per_n = lambda *shape: pl.BlockSpec((None,) + shape, lambda b, c, pt: (b,) + (0,) * len(shape))
    const = lambda *shape: pl.BlockSpec(shape, lambda b, c, pt: (0,) * len(shape))
    in_specs = (_page_specs(256, ppc) * 4 + [
        per_n(64, 256), per_n(64, 256), per_n(64, 128),
        pl.BlockSpec((None, 8, ppc * 128), lambda b, c, pt: (b, 0, c)),
        pl.BlockSpec((None, 8, 128), lambda b, c, pt: (b, 0, n_pages)),
        const(2, 64, 128), const(2, 64, 128),
        per_n(128, 256), per_n(128, 256), per_n(128, 256), per_n(128, 256)])
    grid_spec = pltpu.PrefetchScalarGridSpec(
        num_scalar_prefetch=1, grid=(n, nc), in_specs=in_specs,
        out_specs=(per_n(64, 64), per_n(64, 64)),
        scratch_shapes=[pltpu.VMEM((64, 1), F32), pltpu.VMEM((64, 1), F32), pltpu.VMEM((64, 256), F32),
                        pltpu.VMEM((64, 1), F32), pltpu.VMEM((64, 1), F32), pltpu.VMEM((64, 256), F32)])
    return pl.pallas_call(
        functools.partial(_sample_attn_kernel, ppc=ppc, n_new=n_new),
        grid_spec=grid_spec,
        out_shape=(jax.ShapeDtypeStruct((n, 64, 64), F32), jax.ShapeDtypeStruct((n, 64, 64), F32)),
        compiler_params=_cparams(2), name="sample_paged_attention",
    )(page_table, *([cache_mk] * ppc), *([cache_mv] * ppc), *([cache_dk] * ppc), *([cache_dv] * ppc),
      qa_rows, qb_rows, bm, mb, mb, bias_last, bias_new, ka_new, va_new, kb_new, vb_new)


def _head_token_rows(q, n, s):
    q = q.reshape(n, s, 8, HEAD_DIM).transpose(0, 2, 1, 3)
    q = jnp.pad(q, ((0, 0), (0, 0), (0, 8 - s), (0, 0)))
    return q.reshape(n, 64, HEAD_DIM)


def _block_diag_rows(q_rows):
    n = q_rows.shape[0]
    grp = jnp.arange(64) // 16
    onehot = (grp[:, None] == jnp.arange(4)[None, :]).astype(q_rows.dtype)
    return (q_rows[:, :, None, :] * onehot[None, :, :, None]).reshape(n, 64, 256)


def _pad_new_rows(x, n, s):
    w = x.shape[1]
    return jnp.pad(x.reshape(n, s, w), ((0, 0), (0, 128 - s), (0, 0))).astype(BF16)


def _features_major_pages(cache):
    n_pool, page = cache.shape[:2]
    return jnp.moveaxis(cache.reshape(n_pool, page, -1), 1, 2)


def kernel(x_prompt, x_sample, cache_moba_k, cache_moba_v, cache_dsa_k, cache_dsa_v, cache_idx_k,
           page_table, c_prompt, c_sample, rel_bias, g_attn, g_ffn, w_ada, b_ada, w_in, g_qa, g_ka,
           g_qb, g_kb, g_oa, g_ob, w_out, w_gate, w_up, w_down):
    batch, seq, d = x_prompt.shape
    n_dec, s_dec, _ = x_sample.shape
    depth = w_in.shape[0]
    page = cache_idx_k.shape[2]
    n_pages = page_table.shape[1]
    past = n_pages * page
    assert batch == 1 and d == 1024 and rel_bias.shape == (N_BUCKETS, 16)
    assert seq % ROW_TILE == 0 and (n_dec * s_dec) % ROW_TILE == 0 and s_dec <= 8
    assert page == 128 and past % MOBA_BLOCK == 0 and n_pages % PAGES_PER_STEP == 0
    assert seq // MOBA_BLOCK <= 128 and past // MOBA_BLOCK <= 128 and n_dec % SELECT_BATCH == 0
    rows_s = n_dec * s_dec
    n_sel_p = min(DSA_TOPK, seq // 4)
    n_sel_s = min(DSA_TOPK, (past + s_dec) // 4)
    idx_width = 512
    wi_scale = float(idx_width) ** -0.5

    bias_p = _bias_tiles(rel_bias, _prompt_bias_patterns())
    bias_s = _bias_tiles(rel_bias, _sample_bias_patterns(s_dec, page))
    bias_last = bias_s[:, 0].reshape(2, 64, 128)
    bias_new = bias_s[:, 1].reshape(2, 64, 128)
    bd = jnp.asarray(np.kron(np.eye(8, dtype=np.float32), np.ones((64, 64), np.float32)), BF16)

    xp = x_prompt.reshape(seq, d)
    xs = x_sample.reshape(rows_s, d)
    pad_rows = (-(1 + n_dec)) % 8
    c_all = jnp.concatenate([c_prompt, c_sample, jnp.zeros((pad_rows, d), F32)], axis=0)
    outs_p, outs_s = [], []
    for l in range(depth):
        mod = _modulation(c_all, w_ada[l], b_ada[l][None, :])
        mod_p = [mod[0:1, k * d:(k + 1) * d] for k in range(6)]
        mod_s = [jnp.repeat(mod[1:1 + n_dec, k * d:(k + 1) * d], s_dec, axis=0) for k in range(6)]
        w_main = w_in[l][:, 0:2560].astype(BF16)
        w_small = jnp.pad(w_in[l][:, 2560:2632], ((0, 0), (0, 56))).astype(BF16)
        gq = jnp.stack([jnp.tile(g_qa[l], 8), jnp.tile(g_qb[l], 8)])
        gk = jnp.stack([jnp.tile(g_ka[l], 4), jnp.tile(g_kb[l], 4)])
        goa, gob = g_oa[l][:, None], g_ob[l][:, None]
        wo, wg, wu, wd = (w_out[l].astype(BF16), w_gate[l].astype(BF16), w_up[l].astype(BF16),
                          w_down[l].astype(BF16))

        (qf, ka, va, kb, vb, sm, kab, kbb, vat, vbt, smb, km) = _project(
            xp, mod_p[0], mod_p[1], g_attn[l][None, :], w_main, w_small, bd, gq, gk, wi_scale)
        nb = seq // MOBA_BLOCK
        kmean = jnp.pad(km.reshape(nb, 256), ((0, 128 - nb), (0, 0)))
        oat = _moba_prompt(qf, kab, vat, kmean, bias_p)
        obt = _dsa_prompt(qf, sm, kbb, vbt, smb, bias_p, n_sel_p)
        xp = _out_ffn(xp, oat, obt, mod_p[2], mod_p[3], mod_p[4], mod_p[5], goa, gob,
                      g_ffn[l][None, :], wo, wg, wu, wd)
        outs_p.append((ka.reshape(1, seq, 4, 64), va.reshape(1, seq, 4, 64), kb.reshape(1, seq, 4, 64),
                       vb.reshape(1, seq, 4, 64), sm[:, :64].reshape(1, seq, 64)))

        (qf, ka, va, kb, vb, sm, _, _, _, _, _, _) = _project(
            xs, mod_s[0], mod_s[1], g_attn[l][None, :], w_main, w_small, bd, gq, gk, wi_scale)
        qa_rows = _block_diag_rows(_head_token_rows(qf[:, 0:512], n_dec, s_dec)).astype(BF16)
        qb_rows = _block_diag_rows(_head_token_rows(qf[:, 512:1024], n_dec, s_dec)).astype(BF16)
        qi_rows = _head_token_rows(qf[:, 1024:1536], n_dec, s_dec).astype(BF16)
        w_col = jnp.pad(sm[:, 64:72].reshape(n_dec, s_dec, 8).transpose(0, 2, 1),
                        ((0, 0), (0, 0), (0, 8 - s_dec))).reshape(n_dec, 64, 1)
        cmk = _features_major_pages(cache_moba_k[l])
        cmv = _features_major_pages(cache_moba_v[l])
        cdk = _features_major_pages(cache_dsa_k[l])
        cdv = _features_major_pages(cache_dsa_v[l])
        cik = _features_major_pages(cache_idx_k[l])
        km_s, scores = _sample_scan(page_table, cmk, cik, qi_rows, w_col)
        nbp = past // MOBA_BLOCK
        ki_new_t = jnp.swapaxes(_pad_new_rows(sm[:, :64], n_dec, s_dec), 1, 2)
        bm, mb = _sample_select(scores, km_s, qa_rows, qi_rows, w_col, ki_new_t, nbp, s_dec, n_sel_s)
        oa_s, ob_s = _sample_attend(
            page_table, cmk, cmv, cdk, cdv, qa_rows, qb_rows, bm, mb, bias_last, bias_new,
            _pad_new_rows(ka, n_dec, s_dec), _pad_new_rows(va, n_dec, s_dec),
            _pad_new_rows(kb, n_dec, s_dec), _pad_new_rows(vb, n_dec, s_dec), s_dec)
        to_t = lambda o: o.reshape(n_dec, 8, 8, 64)[:, :, :s_dec].transpose(1, 3, 0, 2).reshape(512, rows_s)
        xs = _out_ffn(xs, to_t(oa_s), to_t(ob_s), mod_s[2], mod_s[3], mod_s[4], mod_s[5], goa, gob,
                      g_ffn[l][None, :], wo, wg, wu, wd)
        outs_s.append((ka.reshape(n_dec, s_dec, 4, 64), va.reshape(n_dec, s_dec, 4, 64),
                       kb.reshape(n_dec, s_dec, 4, 64), vb.reshape(n_dec, s_dec, 4, 64),
                       sm[:, :64].reshape(n_dec, s_dec, 64)))

    stack = lambda outs, k: jnp.stack([o[k] for o in outs])
    return (xp.reshape(batch, seq, d), xs.reshape(n_dec, s_dec, d),
            *[stack(outs_p, k) for k in range(5)], *[stack(outs_s, k) for k in range(5)])
```

```python
import functools
import math

import numpy as np
import jax
import jax.numpy as jnp
from jax import lax
from jax.experimental import pallas as pl
from jax.experimental.pallas import tpu as pltpu

F32 = jnp.float32
BF16 = jnp.bfloat16

HEAD_DIM = 64
MOBA_BLOCK = 256
MOBA_TOPK = 3
DSA_TOPK = 256
N_BUCKETS = 32
MAX_DISTANCE = 128
EPS = 1e-6
LOG2E = math.log2(math.e)
NEG = -1e30
LOWEST = -3.0e38
ROW_TILE = 512
FFN_TILE = 256
TQ = 256
TK = 256
ATTEND_TILES = 4
PAGES_PER_STEP = 16
SELECT_BATCH = 8
MAX_BISECT = 256
VMEM_LIMIT = 60 * 1024 * 1024


def _cparams(n_axes):
    return pltpu.CompilerParams(dimension_semantics=("arbitrary",) * n_axes,
                                vmem_limit_bytes=VMEM_LIMIT)


def _resident(block_shape, index_map):
    return pl.BlockSpec(block_shape, index_map, pipeline_mode=pl.Buffered(1))


def _dot(a, b):
    return jnp.dot(a, b, preferred_element_type=F32)


def _dot_nt(a, b):
    return lax.dot_general(a, b, (((1,), (1,)), ((), ())), preferred_element_type=F32)


def _bucket_np(dist):
    n = np.maximum(dist, 0)
    exact = N_BUCKETS // 2
    ratio = np.log(np.maximum(n, 1).astype(np.float32) / np.float32(exact)) / np.float32(
        math.log(MAX_DISTANCE / exact))
    large = np.minimum(exact + (ratio * np.float32(N_BUCKETS - exact)).astype(np.int32), N_BUCKETS - 1)
    return np.where(n < exact, n, large).astype(np.int32)


def _mod_kernel(c_ref, w_ref, b_ref, o_ref):
    c = c_ref[...]
    s = c * jax.nn.sigmoid(c)
    o_ref[...] = jnp.dot(s, w_ref[...], preferred_element_type=F32,
                         precision=lax.Precision.HIGHEST) + b_ref[...]


def _modulation(c_all, w_ada, b_ada):
    rows, d = c_all.shape
    n6 = w_ada.shape[1]
    tn = 1024
    return pl.pallas_call(
        _mod_kernel,
        grid=(n6 // tn,),
        in_specs=[pl.BlockSpec((rows, d), lambda j: (0, 0)),
                  pl.BlockSpec((d, tn), lambda j: (0, j)),
                  pl.BlockSpec((1, tn), lambda j: (0, j))],
        out_specs=pl.BlockSpec((rows, tn), lambda j: (0, j)),
        out_shape=jax.ShapeDtypeStruct((rows, n6), F32),
        compiler_params=_cparams(1),
        name="adaln_modulation",
    )(c_all, w_ada, b_ada)


def _bias_kernel(rb_ref, pat_ref, o_ref):
    h = pl.program_id(0)
    pat = pat_ref[...]
    base = rb_ref[N_BUCKETS - 1, h]
    out = jnp.zeros(pat.shape, F32)
    for b in range(N_BUCKETS - 1):
        out = jnp.where(pat == b, rb_ref[b, h] - base, out)
    o_ref[0] = out * LOG2E


def _bias_tiles(rel_bias, pattern):
    n_heads = rel_bias.shape[1]
    nd = pattern.ndim
    zeros = (0,) * nd
    return pl.pallas_call(
        _bias_kernel,
        grid=(n_heads,),
        in_specs=[pl.BlockSpec(memory_space=pltpu.SMEM),
                  pl.BlockSpec(pattern.shape, lambda h: zeros)],
        out_specs=pl.BlockSpec((1,) + pattern.shape, lambda h: (h,) + zeros),
        out_shape=jax.ShapeDtypeStruct((n_heads,) + pattern.shape, F32),
        compiler_params=_cparams(1),
        name="rel_bias_tiles",
    )(rel_bias, jnp.asarray(pattern))


def _prompt_bias_patterns():
    key = np.arange(TK)[:, None]
    qry = np.arange(TQ)[None, :]
    diag = np.where(qry >= key, _bucket_np(qry - key), N_BUCKETS - 1)
    adj = _bucket_np(TK + qry - key)
    return np.stack([diag, adj]).astype(np.int32)


def _sample_bias_patterns(n_new, page):
    s = np.arange(8)[:, None]
    c = np.arange(page)[None, :]
    last = _bucket_np(page + s - c)
    new = np.where((c <= s) & (c < n_new), _bucket_np(s - c), N_BUCKETS - 1)
    return np.stack([last, new]).astype(np.int32)


def _proj_kernel(x_ref, sh_ref, sc_ref, g_ref, wm_ref, ws_ref, bd_ref, gq_ref, gk_ref,
                 qf_ref, ka_ref, va_ref, kb_ref, vb_ref, sm_ref,
                 kab_ref, kbb_ref, vat_ref, vbt_ref, smb_ref, km_ref, *, wi_scale):
    x = x_ref[...]
    xn = x * lax.rsqrt(jnp.mean(x * x, axis=-1, keepdims=True) + EPS) * g_ref[...]
    h = (xn * (1.0 + sc_ref[...]) + sh_ref[...]).astype(BF16)
    y = _dot(h, wm_ref[...])
    ys = _dot(h, ws_ref[...])

    def head_norm(t, g, width):
        ss = _dot((t * t).astype(BF16), bd_ref[:width, :width])
        return t * lax.rsqrt(ss * (1.0 / HEAD_DIM) + EPS) * g

    scale = HEAD_DIM ** -0.5 * LOG2E
    qa = head_norm(y[:, 0:512], gq_ref[0:1, :], 512)
    ka = head_norm(y[:, 512:768], gk_ref[0:1, :], 256)
    va = y[:, 768:1024]
    qb = head_norm(y[:, 1024:1536], gq_ref[1:2, :], 512)
    kb = head_norm(y[:, 1536:1792], gk_ref[1:2, :], 256)
    vb = y[:, 1792:2048]
    qf_ref[:, 0:512] = qa * scale
    qf_ref[:, 512:1024] = qb * scale
    qf_ref[:, 1024:1536] = y[:, 2048:2560]
    ka_ref[...] = ka
    va_ref[...] = va
    kb_ref[...] = kb
    vb_ref[...] = vb
    lane = lax.broadcasted_iota(jnp.int32, (1, 128), 1)
    sm = ys * jnp.where((lane >= 64) & (lane < 72), wi_scale, 1.0)
    sm_ref[...] = sm
    means = []
    for b in range(ROW_TILE // TK):
        rows = slice(b * TK, (b + 1) * TK)
        kab_ref[b] = ka[rows].astype(BF16)
        kbb_ref[b] = kb[rows].astype(BF16)
        vat_ref[b] = va[rows].T.astype(BF16)
        vbt_ref[b] = vb[rows].T.astype(BF16)
        smb_ref[b] = sm[rows].astype(BF16)
        means.append(jnp.mean(ka[rows], axis=0, keepdims=True))
    km_ref[...] = jnp.concatenate(means, axis=0)


def _project(x, shift, scale, g_attn, w_main, w_small, bd, gq, gk, wi_scale):
    rows, d = x.shape
    nt = rows // ROW_TILE
    nb = rows // TK
    bpt = ROW_TILE // TK
    mrows = shift.shape[0]
    mod_map = (lambda i: (i, 0)) if mrows == rows else (lambda i: (0, 0))
    mod_blk = ROW_TILE if mrows == rows else 1
    row_spec = lambda w: pl.BlockSpec((ROW_TILE, w), lambda i: (i, 0))
    blk_spec = lambda w: pl.BlockSpec((bpt, TK, w), lambda i: (i, 0, 0))
    out_shape = (
        jax.ShapeDtypeStruct((rows, 1536), F32),
        jax.ShapeDtypeStruct((rows, 256), F32),
        jax.ShapeDtypeStruct((rows, 256), F32),
        jax.ShapeDtypeStruct((rows, 256), F32),
        jax.ShapeDtypeStruct((rows, 256), F32),
        jax.ShapeDtypeStruct((rows, 128), F32),
        jax.ShapeDtypeStruct((nb, TK, 256), BF16),
        jax.ShapeDtypeStruct((nb, TK, 256), BF16),
        jax.ShapeDtypeStruct((nb, 256, TK), BF16),
        jax.ShapeDtypeStruct((nb, 256, TK), BF16),
        jax.ShapeDtypeStruct((nb, TK, 128), BF16),
        jax.ShapeDtypeStruct((nt, bpt, 256), F32),
    )
    out_specs = (row_spec(1536), row_spec(256), row_spec(256), row_spec(256), row_spec(256),
                 row_spec(128), blk_spec(256), blk_spec(256), blk_spec(TK), blk_spec(TK),
                 blk_spec(128), pl.BlockSpec((None, bpt, 256), lambda i: (i, 0, 0)))
    in_specs = [row_spec(d),
                pl.BlockSpec((mod_blk, d), mod_map),
                pl.BlockSpec((mod_blk, d), mod_map),
                pl.BlockSpec((1, d), lambda i: (0, 0)),
                _resident(w_main.shape, lambda i: (0, 0)),
                _resident(w_small.shape, lambda i: (0, 0)),
                _resident(bd.shape, lambda i: (0, 0)),
                pl.BlockSpec(gq.shape, lambda i: (0, 0)),
                pl.BlockSpec(gk.shape, lambda i: (0, 0))]
    return pl.pallas_call(
        functools.partial(_proj_kernel, wi_scale=wi_scale),
        grid=(nt,), in_specs=in_specs, out_specs=out_specs, out_shape=out_shape,
        compiler_params=_cparams(1), name="adaln_in_projection",
    )(x, shift, scale, g_attn, w_main, w_small, bd, gq, gk)


def _store_padded_queries(qp_ref, q_ref):
    qT = q_ref[...].T
    zeros = jnp.zeros((64, 2 * TQ), BF16)
    for g in range(4):
        q2 = jnp.concatenate([qT[128 * g:128 * g + 64], qT[128 * g + 64:128 * g + 128]],
                             axis=1).astype(BF16)
        qp_ref[g] = jnp.concatenate([q2 if gg == g else zeros for gg in range(4)], axis=0)


def _for_tile_pairs(n, body, carry, per=2):
    start = 0
    while per >= 1:
        trips = (n - start) // per
        carry = lax.fori_loop(
            0, trips, lambda t, c, s=start, p=per: body([s + p * t + k for k in range(p)], c), carry)
        start = start + per * trips
        per //= 2
    return carry


def _attend(i, qp_ref, k_ref, vt_ref, bias_ref, o_ref, far_masks, diag_masks):
    def tile_update(tiles, carry, tile_masks, near):
        kt = jnp.concatenate([k_ref[j] for j in tiles], axis=0)
        logits = [_dot(kt, qp_ref[g]) for g in range(4)]
        out = []
        for g in range(4):
            m, l, acc = carry[3 * g:3 * g + 3]
            mask = jnp.concatenate([jnp.broadcast_to(tm[g], (TK, 2 * TQ)) for tm in tile_masks], axis=0)
            s = logits[g] + mask
            if near is not None:
                s = s + jnp.concatenate([bias_ref[2 * g, near], bias_ref[2 * g + 1, near]], axis=1)
            m_new = jnp.maximum(m, jnp.max(s, axis=0, keepdims=True))
            if g < 3:
                m_new = jnp.maximum(m_new, logits[g + 1][0:1, :] * 0.0 + NEG)
            alpha = jnp.exp2(m - m_new)
            p = jnp.exp2(s - m_new)
            l = alpha * l + jnp.sum(p, axis=0, keepdims=True)
            vt = jnp.concatenate([vt_ref[j, 64 * g:64 * g + 64, :] for j in tiles], axis=1)
            acc = alpha * acc + _dot(vt, p.astype(BF16))
            out.extend((m_new, l, acc))
        return tuple(out)

    carry = (jnp.full((1, 2 * TQ), NEG, F32), jnp.zeros((1, 2 * TQ), F32),
             jnp.zeros((HEAD_DIM, 2 * TQ), F32)) * 4
    n_far = jnp.maximum(i - 1, 0)
    carry = _for_tile_pairs(n_far, lambda tiles, c: tile_update(tiles, c, [far_masks(j) for j in tiles], None),
                            carry, per=ATTEND_TILES)
    carry = lax.fori_loop(n_far, i, lambda j, c: tile_update([j], c, [far_masks(j)], 1), carry)
    carry = tile_update([i], carry, [diag_masks()], 0)
    for g in range(4):
        out = carry[3 * g + 2] / carry[3 * g + 1]
        o_ref[128 * g:128 * g + 64, :] = out[:, :TQ]
        o_ref[128 * g + 64:128 * g + 128, :] = out[:, TQ:]


def _kth_largest(count_ge, counts_init, rmin, rmax, n_adm, n_sel):
    ksel = jnp.float32(n_sel)
    c_top, c_zero, c_pos = counts_init
    few = n_adm <= ksel
    top = ~few & (c_top >= ksel)
    zero = ~few & ~top & (c_pos < ksel) & (c_zero >= ksel)
    pos = c_pos >= ksel
    active = jnp.where(few | top | zero, 0.0, 1.0)
    tau = jnp.where(few, LOWEST, jnp.where(top, rmax, 0.0))
    check = jnp.where(top | zero, 1.0, 0.0)
    lo = jnp.where(pos, 0.0, rmin)
    hi = jnp.where(pos, rmax, 0.0)

    def cond(st):
        return (st[0] < MAX_BISECT) & (jnp.sum(st[1]) > 0.0)

    def step(st):
        it, active, lo, hi, tau, check = st
        mid = lo + (hi - lo) * 0.5
        stuck = (mid <= lo) | (mid >= hi)
        c = count_ge(mid)
        act = active > 0.0
        found = act & ~stuck & (c == ksel)
        ended = act & stuck
        ge = c >= ksel
        tau = jnp.where(found, mid, jnp.where(ended, lo, tau))
        check = jnp.where(ended, 1.0, check)
        lo = jnp.where(ge, mid, lo)
        hi = jnp.where(ge, hi, mid)
        active = jnp.where(found | ended, 0.0, active)
        return it + 1, active, lo, hi, tau, check

    _, active, lo, _, tau, check = lax.while_loop(
        cond, step, (jnp.int32(0), active, lo, hi, tau, check))
    left = active > 0.0
    return jnp.where(left, lo, tau), jnp.where(left, 1.0, check)


def _moba_kernel(q_ref, k_ref, vt_ref, km_ref, bias_ref, o_ref, qp_ref, mc_ref):
    i = pl.program_id(0)
    _store_padded_queries(qp_ref, q_ref)
    blk = lax.broadcasted_iota(jnp.int32, (128, 2 * TQ), 0).astype(F32)
    valid = blk < i.astype(F32)
    kmb = km_ref[...].astype(BF16)
    for g in range(4):
        gate = jnp.where(valid, _dot(kmb, qp_ref[g]), -jnp.inf)
        sel = jnp.zeros_like(gate)
        for _ in range(MOBA_TOPK):
            mx = jnp.max(gate, axis=0, keepdims=True)
            first = jnp.min(jnp.where(gate == mx, blk, 1e9), axis=0, keepdims=True)
            hit = blk == first
            sel = jnp.where(hit & valid, 1.0, sel)
            gate = jnp.where(hit, -jnp.inf, gate)
        mc_ref[g] = jnp.where(sel > 0.0, 0.0, NEG)
    key_io = lax.broadcasted_iota(jnp.int32, (TK, 2 * TQ), 0)
    qry_io = lax.broadcasted_iota(jnp.int32, (TK, 2 * TQ), 1) % TQ
    causal = jnp.where(key_io <= qry_io, 0.0, NEG)
    _attend(i, qp_ref, k_ref, vt_ref, bias_ref, o_ref,
            far_masks=lambda j: [mc_ref[g, pl.ds(j, 1), :] for g in range(4)],
            diag_masks=lambda: [causal] * 4)


def _moba_prompt(qf, kab, vat, kmean, bias):
    nb = kab.shape[0]
    t = nb * TK
    return pl.pallas_call(
        _moba_kernel,
        grid=(t // TQ,),
        in_specs=[pl.BlockSpec((TQ, 512), lambda i: (i, 0)),
                  _resident(kab.shape, lambda i: (0, 0, 0)),
                  _resident(vat.shape, lambda i: (0, 0, 0)),
                  _resident(kmean.shape, lambda i: (0, 0)),
                  _resident((8, 2, TK, TQ), lambda i: (0, 0, 0, 0))],
        out_specs=pl.BlockSpec((512, TQ), lambda i: (0, i)),
        out_shape=jax.ShapeDtypeStruct((512, t), F32),
        scratch_shapes=[pltpu.VMEM((4, 256, 2 * TQ), BF16), pltpu.VMEM((4, 128, 2 * TQ), F32)],
        compiler_params=_cparams(1), name="moba_prompt_attention",
    )(qf, kab, vat, kmean, bias)


def _fold8(x):
    return jnp.sum(x.reshape(TK // 8, 8, TQ), axis=0)


def _dsa_kernel(qb_ref, qi_ref, sm_ref, k_ref, vt_ref, ksm_ref, bias_ref, o_ref, qp_ref, sc_ref,
                *, n_sel):
    i = pl.program_id(0)
    qiT = qi_ref[...].T
    zeros = jnp.zeros((64, TQ), F32)
    qi_all = jnp.concatenate(
        [jnp.concatenate([qiT[64 * h:64 * h + 64], zeros], axis=0) for h in range(8)],
        axis=1).astype(BF16)
    wT = sm_ref[...].T
    w_rows = [wT[64 + h:65 + h, :] for h in range(8)]
    key_io = lax.broadcasted_iota(jnp.int32, (TK, TQ), 0)
    qry_io = lax.broadcasted_iota(jnp.int32, (TK, TQ), 1)
    causal = key_io <= qry_io

    def score_tile(j):
        z = _dot(ksm_ref[j], qi_all)
        sc = w_rows[0] * jnp.maximum(z[:, 0:TQ], 0.0)
        for h in range(1, 8):
            sc = sc + w_rows[h] * jnp.maximum(z[:, TQ * h:TQ * (h + 1)], 0.0)
        return sc

    def fold_max(x):
        return jnp.max(x.reshape(TK // 8, 8, TQ), axis=0)

    def stats(carry, hi, lo):
        mx, mn, c_zero, c_pos = carry
        return (jnp.maximum(mx, fold_max(hi)), jnp.minimum(mn, -fold_max(-lo)),
                c_zero + _fold8(jnp.where(hi >= 0.0, 1.0, 0.0)),
                c_pos + _fold8(jnp.where(hi > 0.0, 1.0, 0.0)))

    def fill(tiles, carry):
        for j in tiles:
            sc = score_tile(j)
            sc_ref[j] = sc
            carry = stats(carry, sc, sc)
        return carry

    carry = _for_tile_pairs(i, fill, (jnp.full((8, TQ), -jnp.inf, F32), jnp.full((8, TQ), jnp.inf, F32),
                                      jnp.zeros((8, TQ), F32), jnp.zeros((8, TQ), F32)))
    scd = score_tile(i)
    diag = jnp.where(causal, scd, -jnp.inf)
    sc_ref[i] = diag
    mx, mn, c_zero, c_pos = stats(carry, diag, jnp.where(causal, scd, jnp.inf))
    rmax = jnp.max(mx, axis=0, keepdims=True)
    rmin = jnp.min(mn, axis=0, keepdims=True)

    def count(pred):
        def body(tiles, acc):
            for j in tiles:
                acc = acc + _fold8(jnp.where(pred(sc_ref[j]), 1.0, 0.0))
            return acc
        return jnp.sum(_for_tile_pairs(i + 1, body, jnp.zeros((8, TQ), F32)), axis=0, keepdims=True)

    init = (count(lambda x: x >= rmax), jnp.sum(c_zero, axis=0, keepdims=True),
            jnp.sum(c_pos, axis=0, keepdims=True))
    n_adm = (i * TQ + 1).astype(F32) + lax.broadcasted_iota(jnp.int32, (1, TQ), 1).astype(F32)
    tau, check = _kth_largest(lambda t: count(lambda x: x >= t), init, rmin, rmax, n_adm, n_sel)

    @pl.when(jnp.sum(check) > 0.0)
    def _():
        need = jnp.float32(n_sel) - count(lambda x: x > tau)
        lower = jnp.where(lax.broadcasted_iota(jnp.int32, (TK, TK), 0)
                          >= lax.broadcasted_iota(jnp.int32, (TK, TK), 1), 1.0, 0.0).astype(BF16)

        def fix(j, seen):
            t = sc_ref[j]
            eq = jnp.where((t == tau) & (check > 0.0), 1.0, 0.0)
            rank = _dot(lower, eq.astype(BF16)) + seen
            sc_ref[j] = jnp.where((eq > 0.0) & (rank > need), -jnp.inf, t)
            return seen + jnp.sum(_fold8(eq), axis=0, keepdims=True)

        lax.fori_loop(0, i + 1, fix, jnp.zeros((1, TQ), F32))

    _store_padded_queries(qp_ref, qb_ref)

    def sel_masks(j):
        mb = jnp.where(sc_ref[j] >= tau, 0.0, NEG)
        return [jnp.concatenate([mb, mb], axis=1)] * 4

    _attend(i, qp_ref, k_ref, vt_ref, bias_ref, o_ref,
            far_masks=sel_masks, diag_masks=lambda: sel_masks(i))


def _dsa_prompt(qf, sm, kbb, vbt, smb, bias, n_sel):
    nb = kbb.shape[0]
    t = nb * TK
    return pl.pallas_call(
        functools.partial(_dsa_kernel, n_sel=n_sel),
        grid=(t // TQ,),
        in_specs=[pl.BlockSpec((TQ, 512), lambda i: (i, 1)),
                  pl.BlockSpec((TQ, 512), lambda i: (i, 2)),
                  pl.BlockSpec((TQ, 128), lambda i: (i, 0)),
                  _resident(kbb.shape, lambda i: (0, 0, 0)),
                  _resident(vbt.shape, lambda i: (0, 0, 0)),
                  _resident(smb.shape, lambda i: (0, 0, 0)),
                  _resident((8, 2, TK, TQ), lambda i: (1, 0, 0, 0))],
        out_specs=pl.BlockSpec((512, TQ), lambda i: (0, i)),
        out_shape=jax.ShapeDtypeStruct((512, t), F32),
        scratch_shapes=[pltpu.VMEM((4, 256, 2 * TQ), BF16), pltpu.VMEM((nb, TK, TQ), F32)],
        compiler_params=_cparams(1), name="dsa_prompt_attention",
    )(qf, qf, sm, kbb, vbt, smb, bias)


def _ffn_kernel(x_ref, oa_ref, ob_ref, gt1_ref, sh2_ref, sc2_ref, gt2_ref, goa_ref, gob_ref, gf_ref,
                wo_ref, wg_ref, wu_ref, wd_ref, y_ref):
    def feat_norm(o, g):
        return o * lax.rsqrt(jnp.mean(o * o, axis=0, keepdims=True) + EPS) * g

    cat = jnp.concatenate([feat_norm(oa_ref[...], goa_ref[...]),
                           feat_norm(ob_ref[...], gob_ref[...])], axis=0)
    att = _dot(cat.T.astype(BF16), wo_ref[...])
    x1 = x_ref[...] + gt1_ref[...] * att
    xn = x1 * lax.rsqrt(jnp.mean(x1 * x1, axis=-1, keepdims=True) + EPS) * gf_ref[...]
    h = (xn * (1.0 + sc2_ref[...]) + sh2_ref[...]).astype(BF16)
    gate = _dot(h, wg_ref[...])
    up = _dot(h, wu_ref[...])
    act = (gate * jax.nn.sigmoid(gate) * up).astype(BF16)
    y_ref[...] = x1 + gt2_ref[...] * _dot(act, wd_ref[...])


def _out_ffn(x, oat, obt, gt1, sh2, sc2, gt2, goa, gob, g_ffn, wo, wg, wu, wd):
    rows, d = x.shape
    mrows = gt1.shape[0]
    mod_map = (lambda i: (i, 0)) if mrows == rows else (lambda i: (0, 0))
    mod_blk = FFN_TILE if mrows == rows else 1
    mod_spec = pl.BlockSpec((mod_blk, d), mod_map)
    col_spec = pl.BlockSpec((512, FFN_TILE), lambda i: (0, i))
    return pl.pallas_call(
        _ffn_kernel,
        grid=(rows // FFN_TILE,),
        in_specs=[pl.BlockSpec((FFN_TILE, d), lambda i: (i, 0)), col_spec, col_spec,
                  mod_spec, mod_spec, mod_spec, mod_spec,
                  pl.BlockSpec((512, 1), lambda i: (0, 0)),
                  pl.BlockSpec((512, 1), lambda i: (0, 0)),
                  pl.BlockSpec((1, d), lambda i: (0, 0)),
                  _resident(wo.shape, lambda i: (0, 0)),
                  _resident(wg.shape, lambda i: (0, 0)),
                  _resident(wu.shape, lambda i: (0, 0)),
                  _resident(wd.shape, lambda i: (0, 0))],
        out_specs=pl.BlockSpec((FFN_TILE, d), lambda i: (i, 0)),
        out_shape=jax.ShapeDtypeStruct((rows, d), F32),
        compiler_params=_cparams(1), name="out_projection_ffn",
    )(x, oat, obt, gt1, sh2, sc2, gt2, goa, gob, g_ffn, wo, wg, wu, wd)


def _stream_pages(pt_ref, hbm_refs, buf_refs, sem_ref, ppc, nc, chunk_body):
    n = pl.program_id(0)
    n_last = pl.num_programs(0) - 1

    def copy(a, page, slot, p):
        return pltpu.make_async_copy(hbm_refs[a].at[page], buf_refs[a].at[slot, p], sem_ref.at[slot, a])

    def start(batch, c, slot):
        for a in range(len(hbm_refs)):
            for p in range(ppc):
                copy(a, pt_ref[batch, c * ppc + p], slot, p).start()

    def wait(slot):
        for a in range(len(hbm_refs)):
            for p in range(ppc):
                copy(a, 0, slot, p).wait()

    @pl.when(n == 0)
    def _():
        start(0, 0, 0)

    for c in range(nc):
        slot = c % 2
        if c + 1 < nc:
            start(n, c + 1, 1 - slot)
        else:
            @pl.when(n < n_last)
            def _():
                start(n + 1, 0, 0)
        wait(slot)
        chunk_body(c, [buf.at[slot] for buf in buf_refs])


def _head_sum(z):
    out = z[0:8]
    for h in range(1, 8):
        out = out + z[8 * h:8 * h + 8]
    return out


def _sample_scan_kernel(pt_ref, mk_hbm, ik_hbm, qi_ref, w_ref, km_ref, sc_ref, mk_buf, ik_buf, sem,
                        *, ppc, nc):
    ppb = MOBA_BLOCK // 128
    bpc = ppc // ppb
    lane = lax.broadcasted_iota(jnp.int32, (256, 128), 1)
    qi = qi_ref[...]
    w = w_ref[...]
    km_ref[...] = jnp.zeros(km_ref.shape, F32)

    def chunk(c, bufs):
        mk, ik = bufs
        km = km_ref[...]
        for b in range(bpc):
            ksum = mk[ppb * b]
            for p in range(1, ppb):
                ksum = ksum + mk[ppb * b + p]
            kmean = jnp.sum(ksum, axis=1, keepdims=True) * (1.0 / MOBA_BLOCK)
            km = km + jnp.where(lane == c * bpc + b, kmean, 0.0)
        km_ref[...] = km
        for p in range(ppc):
            z = _dot(qi, ik[p].astype(BF16))
            col = (c * ppc + p) * 128
            sc_ref[:, col:col + 128] = _head_sum(jnp.maximum(z, 0.0) * w)

    _stream_pages(pt_ref, [mk_hbm, ik_hbm], [mk_buf, ik_buf], sem, ppc, nc, chunk)


def _sample_scan(page_table, cache_mk, cache_ik, qi_rows, w_col):
    n, n_pages = page_table.shape
    ppc = PAGES_PER_STEP
    nc = n_pages // ppc
    any_spec = pl.BlockSpec(memory_space=pl.ANY)
    grid_spec = pltpu.PrefetchScalarGridSpec(
        num_scalar_prefetch=1, grid=(n,),
        in_specs=[any_spec, any_spec,
                  pl.BlockSpec((None, 64, 64), lambda b, pt: (b, 0, 0)),
                  pl.BlockSpec((None, 64, 1), lambda b, pt: (b, 0, 0))],
        out_specs=(pl.BlockSpec((None, 256, 128), lambda b, pt: (b, 0, 0)),
                   pl.BlockSpec((None, 8, n_pages * 128), lambda b, pt: (b, 0, 0))),
        scratch_shapes=[pltpu.VMEM((2, ppc, 256, 128), F32), pltpu.VMEM((2, ppc, 64, 128), F32),
                        pltpu.SemaphoreType.DMA((2, 2))])
    return pl.pallas_call(
        functools.partial(_sample_scan_kernel, ppc=ppc, nc=nc),
        grid_spec=grid_spec,
        out_shape=(jax.ShapeDtypeStruct((n, 256, 128), F32),
                   jax.ShapeDtypeStruct((n, 8, n_pages * 128), F32)),
        compiler_params=_cparams(1), name="sample_page_scan",
    )(page_table, cache_mk, cache_ik, qi_rows, w_col)


def _sample_select_kernel(sc_ref, km_ref, qa_ref, qi_ref, w_ref, kin_ref, bm_ref, mb_ref,
                          *, n_blocks, n_new, n_sel, idx_iter):
    nb = SELECT_BATCH
    rows = nb * 64
    lane = lax.broadcasted_iota(jnp.int32, (rows, 128), 1).astype(F32)
    valid = lane < float(n_blocks)
    gate = jnp.concatenate([_dot(qa_ref[b], km_ref[b].astype(BF16)) for b in range(nb)], axis=0)
    gate = jnp.where(valid, gate, -jnp.inf)
    sel = jnp.zeros_like(gate)
    for _ in range(min(MOBA_TOPK, n_blocks)):
        mx = jnp.max(gate, axis=1, keepdims=True)
        first = jnp.min(jnp.where(gate == mx, lane, 1e9), axis=1, keepdims=True)
        hit = lane == first
        sel = jnp.where(hit & valid, 1.0, sel)
        gate = jnp.where(hit, -jnp.inf, gate)
    bm_ref[...] = jnp.where(sel > 0.0, 0.0, NEG).reshape(nb, 64, 128)

    new = jnp.concatenate(
        [_head_sum(jnp.maximum(_dot(qi_ref[b], kin_ref[b]), 0.0) * w_ref[b]) for b in range(nb)], axis=0)
    tok = lax.broadcasted_iota(jnp.int32, (nb * 8, 128), 0) % 8
    col = lax.broadcasted_iota(jnp.int32, (nb * 8, 128), 1)
    new = jnp.where((col <= tok) & (col < n_new), new, -jnp.inf)
    past = sc_ref.shape[-1]
    a = jnp.concatenate([sc_ref[...].reshape(nb * 8, past), new], axis=1)
    width = past + 128
    idx = lax.broadcasted_iota(jnp.int32, (nb * 8, width), 1).astype(F32)
    adm = a > -jnp.inf

    def count(mask):
        return jnp.sum(jnp.where(mask, 1.0, 0.0), axis=1, keepdims=True)

    rmax = jnp.max(a, axis=1, keepdims=True)
    rmin = jnp.min(jnp.where(adm, a, jnp.inf), axis=1, keepdims=True)
    init = (count(a >= rmax), count(a >= 0.0), count(a > 0.0))
    tau, check = _kth_largest(lambda t: count(a >= t), init, rmin, rmax, count(adm), n_sel)

    eq = a == tau
    need = jnp.float32(n_sel) - count(a > tau)

    def bis(_, carry):
        lj, hj = carry
        mj = jnp.floor((lj + hj) * 0.5)
        ok = count(eq & (idx <= mj)) >= need
        return jnp.where(ok, lj, mj), jnp.where(ok, mj, hj)

    _, hj = lax.fori_loop(0, idx_iter, bis, (jnp.full((nb * 8, 1), -1.0, F32),
                                             jnp.full((nb * 8, 1), float(width - 1), F32)))
    hj = jnp.where(check > 0.0, hj, float(width))
    keep = (a > tau) | (eq & (idx <= hj))
    mb_ref[...] = jnp.where(keep, 0.0, NEG).reshape(nb, 8, width)


def _sample_select(scores, kmean_t, qa_rows, qi_rows, w_col, ki_new_t, n_blocks, n_new, n_sel):
    n, _, past = scores.shape
    nb = SELECT_BATCH
    width = past + 128
    kern = functools.partial(_sample_select_kernel, n_blocks=n_blocks, n_new=n_new, n_sel=n_sel,
                             idx_iter=int(math.ceil(math.log2(width))) + 1)
    per_n = lambda *shape: pl.BlockSpec((nb,) + shape, lambda b: (b,) + (0,) * len(shape))
    return pl.pallas_call(
        kern, grid=(n // nb,),
        in_specs=[per_n(8, past), per_n(256, 128), per_n(64, 256), per_n(64, 64), per_n(64, 1),
                  per_n(64, 128)],
        out_specs=(per_n(64, 128), per_n(8, width)),
        out_shape=(jax.ShapeDtypeStruct((n, 64, 128), F32),
                   jax.ShapeDtypeStruct((n, 8, width), F32)),
        compiler_params=_cparams(1), name="sample_selection",
    )(scores, kmean_t, qa_rows, qi_rows, w_col, ki_new_t)


def _row_update(s, pv, state):
    m, l, acc = state
    m_new = jnp.maximum(m, jnp.max(s, axis=1, keepdims=True))
    alpha = jnp.exp2(m - m_new)
    p = jnp.exp2(s - m_new)
    return (m_new, alpha * l + jnp.sum(p, axis=1, keepdims=True), alpha * acc + pv(p.astype(BF16)))


def _sample_attn_kernel(pt_ref, mk_hbm, mv_hbm, dk_hbm, dv_hbm, qa_ref, qb_ref, bm_ref, mb_ref, bl_ref,
                        bn_ref, kan_ref, van_ref, kbn_ref, vbn_ref, oa_ref, ob_ref,
                        mk_buf, mv_buf, dk_buf, dv_buf, sem, *, ppc, nc, n_new):
    ppb = MOBA_BLOCK // 128
    bpc = ppc // ppb
    lane = lax.broadcasted_iota(jnp.int32, (64, 128), 1)
    tile8 = lambda x: jnp.concatenate([x] * 8, axis=0)
    qa = qa_ref[...]
    qb = qb_ref[...]
    bm = bm_ref[...]
    fresh = (jnp.full((64, 1), NEG, F32), jnp.zeros((64, 1), F32), jnp.zeros((64, 256), F32))
    states = [fresh, fresh]

    def logits(q, pages):
        return jnp.concatenate([_dot(q, pages[p].astype(BF16)) for p in range(ppc)], axis=1)

    def values(pages):
        def pv(p):
            out = _dot_nt(p[:, 0:128], pages[0].astype(BF16))
            for k in range(1, ppc):
                out = out + _dot_nt(p[:, 128 * k:128 * (k + 1)], pages[k].astype(BF16))
            return out
        return pv

    def chunk(c, bufs):
        mk, mv, dk, dv = bufs
        gate = jnp.concatenate(
            [jnp.broadcast_to(jnp.sum(jnp.where(lane == c * bpc + b, bm, 0.0), axis=1, keepdims=True),
                              (64, MOBA_BLOCK)) for b in range(bpc)], axis=1)
        sa = logits(qa, mk) + gate
        sb = logits(qb, dk) + tile8(mb_ref[:, c * ppc * 128:(c + 1) * ppc * 128])
        if c == nc - 1:
            pad = jnp.zeros((64, (ppc - 1) * 128), F32)
            sa = sa + jnp.concatenate([pad, bl_ref[0]], axis=1)
            sb = sb + jnp.concatenate([pad, bl_ref[1]], axis=1)
        states[0] = _row_update(sa, values(mv), states[0])
        states[1] = _row_update(sb, values(dv), states[1])

    _stream_pages(pt_ref, [mk_hbm, mv_hbm, dk_hbm, dv_hbm], [mk_buf, mv_buf, dk_buf, dv_buf], sem,
                  ppc, nc, chunk)

    tok = lax.broadcasted_iota(jnp.int32, (64, 128), 0) % 8
    own = (lane <= tok) & (lane < n_new)
    s = jnp.where(own, _dot_nt(qa, kan_ref[...]) + bn_ref[0], NEG)
    states[0] = _row_update(s, lambda p: _dot(p, van_ref[...]), states[0])
    s = _dot_nt(qb, kbn_ref[...]) + bn_ref[1] + tile8(mb_ref[:, nc * ppc * 128:nc * ppc * 128 + 128])
    states[1] = _row_update(s, lambda p: _dot(p, vbn_ref[...]), states[1])
    grp = lax.broadcasted_iota(jnp.int32, (64, 64), 0) // 16
    for o_ref, (m, l, acc) in ((oa_ref, states[0]), (ob_ref, states[1])):
        o = acc / l
        out = jnp.zeros((64, 64), F32)
        for g in range(4):
            out = out + jnp.where(grp == g, o[:, 64 * g:64 * g + 64], 0.0)
        o_ref[...] = out


def _sample_attend(page_table, cache_mk, cache_mv, cache_dk, cache_dv, qa_rows, qb_rows, bm, mb,
                   bias_last, bias_new, ka_new, va_new, kb_new, vb_new, n_new):
    n, n_pages = page_table.shape
    ppc = PAGES_PER_STEP
    nc = n_pages // ppc
    width = mb.shape[-1]
    any_spec = pl.BlockSpec(memory_space=pl.ANY)
    per_n = lambda *shape: pl.BlockSpec((None,) + shape, lambda b, pt: (b,) + (0,) * len(shape))
    const = lambda *shape: pl.BlockSpec(shape, lambda b, pt: (0,) * len(shape))
    in_specs = [any_spec] * 4 + [
        per_n(64, 256), per_n(64, 256), per_n(64, 128), per_n(8, width),
        const(2, 64, 128), const(2, 64, 128),
        per_n(128, 256), per_n(128, 256), per_n(128, 256), per_n(128, 256)]
    grid_spec = pltpu.PrefetchScalarGridSpec(
        num_scalar_prefetch=1, grid=(n,), in_specs=in_specs,
        out_specs=(per_n(64, 64), per_n(64, 64)),
        scratch_shapes=[pltpu.VMEM((2, ppc, 256, 128), F32)] * 4 + [pltpu.SemaphoreType.DMA((2, 4))])
    return pl.pallas_call(
        functools.partial(_sample_attn_kernel, ppc=ppc, nc=nc, n_new=n_new),
        grid_spec=grid_spec,
        out_shape=(jax.ShapeDtypeStruct((n, 64, 64), F32), jax.ShapeDtypeStruct((n, 64, 64), F32)),
        compiler_params=_cparams(1), name="sample_paged_attention",
    )(page_table, cache_mk, cache_mv, cache_dk, cache_dv,
      qa_rows, qb_rows, bm, mb, bias_last, bias_new, ka_new, va_new, kb_new, vb_new)


def _head_token_rows(q, n, s):
    q = q.reshape(n, s, 8, HEAD_DIM).transpose(0, 2, 1, 3)
    q = jnp.pad(q, ((0, 0), (0, 0), (0, 8 - s), (0, 0)))
    return q.reshape(n, 64, HEAD_DIM)


def _block_diag_rows(q_rows):
    n = q_rows.shape[0]
    grp = jnp.arange(64) // 16
    onehot = (grp[:, None] == jnp.arange(4)[None, :]).astype(q_rows.dtype)
    return (q_rows[:, :, None, :] * onehot[None, :, :, None]).reshape(n, 64, 256)


def _pad_new_rows(x, n, s):
    w = x.shape[1]
    return jnp.pad(x.reshape(n, s, w), ((0, 0), (0, 128 - s), (0, 0))).astype(BF16)


def _features_major_pages(cache):
    n_pool, page = cache.shape[:2]
    return jnp.moveaxis(cache.reshape(n_pool, page, -1), 1, 2)


def kernel(x_prompt, x_sample, cache_moba_k, cache_moba_v, cache_dsa_k, cache_dsa_v, cache_idx_k,
           page_table, c_prompt, c_sample, rel_bias, g_attn, g_ffn, w_ada, b_ada, w_in, g_qa, g_ka,
           g_qb, g_kb, g_oa, g_ob, w_out, w_gate, w_up, w_down):
    batch, seq, d = x_prompt.shape
    n_dec, s_dec, _ = x_sample.shape
    depth = w_in.shape[0]
    page = cache_idx_k.shape[2]
    n_pages = page_table.shape[1]
    past = n_pages * page
    assert batch == 1 and d == 1024 and rel_bias.shape == (N_BUCKETS, 16)
    assert seq % ROW_TILE == 0 and (n_dec * s_dec) % ROW_TILE == 0 and s_dec <= 8
    assert page == 128 and past % MOBA_BLOCK == 0 and n_pages % (2 * PAGES_PER_STEP) == 0
    assert seq // MOBA_BLOCK <= 128 and past // MOBA_BLOCK <= 128 and n_dec % SELECT_BATCH == 0
    rows_s = n_dec * s_dec
    n_sel_p = min(DSA_TOPK, seq // 4)
    n_sel_s = min(DSA_TOPK, (past + s_dec) // 4)
    idx_width = 512
    wi_scale = float(idx_width) ** -0.5

    bias_p = _bias_tiles(rel_bias, _prompt_bias_patterns())
    bias_s = _bias_tiles(rel_bias, _sample_bias_patterns(s_dec, page))
    bias_last = bias_s[:, 0].reshape(2, 64, 128)
    bias_new = bias_s[:, 1].reshape(2, 64, 128)
    bd = jnp.asarray(np.kron(np.eye(8, dtype=np.float32), np.ones((64, 64), np.float32)), BF16)

    xp = x_prompt.reshape(seq, d)
    xs = x_sample.reshape(rows_s, d)
    pad_rows = (-(1 + n_dec)) % 8
    c_all = jnp.concatenate([c_prompt, c_sample, jnp.zeros((pad_rows, d), F32)], axis=0)
    outs_p, outs_s = [], []
    for l in range(depth):
        mod = _modulation(c_all, w_ada[l], b_ada[l][None, :])
        mod_p = [mod[0:1, k * d:(k + 1) * d] for k in range(6)]
        mod_s = [jnp.repeat(mod[1:1 + n_dec, k * d:(k + 1) * d], s_dec, axis=0) for k in range(6)]
        w_main = w_in[l][:, 0:2560].astype(BF16)
        w_small = jnp.pad(w_in[l][:, 2560:2632], ((0, 0), (0, 56))).astype(BF16)
        gq = jnp.stack([jnp.tile(g_qa[l], 8), jnp.tile(g_qb[l], 8)])
        gk = jnp.stack([jnp.tile(g_ka[l], 4), jnp.tile(g_kb[l], 4)])
        goa, gob = g_oa[l][:, None], g_ob[l][:, None]
        wo, wg, wu, wd = (w_out[l].astype(BF16), w_gate[l].astype(BF16), w_up[l].astype(BF16),
                          w_down[l].astype(BF16))

        (qf, ka, va, kb, vb, sm, kab, kbb, vat, vbt, smb, km) = _project(
            xp, mod_p[0], mod_p[1], g_attn[l][None, :], w_main, w_small, bd, gq, gk, wi_scale)
        nb = seq // MOBA_BLOCK
        kmean = jnp.pad(km.reshape(nb, 256), ((0, 128 - nb), (0, 0)))
        oat = _moba_prompt(qf, kab, vat, kmean, bias_p)
        obt = _dsa_prompt(qf, sm, kbb, vbt, smb, bias_p, n_sel_p)
        xp = _out_ffn(xp, oat, obt, mod_p[2], mod_p[3], mod_p[4], mod_p[5], goa, gob,
                      g_ffn[l][None, :], wo, wg, wu, wd)
        outs_p.append((ka.reshape(1, seq, 4, 64), va.reshape(1, seq, 4, 64), kb.reshape(1, seq, 4, 64),
                       vb.reshape(1, seq, 4, 64), sm[:, :64].reshape(1, seq, 64)))

        (qf, ka, va, kb, vb, sm, _, _, _, _, _, _) = _project(
            xs, mod_s[0], mod_s[1], g_attn[l][None, :], w_main, w_small, bd, gq, gk, wi_scale)
        qa_rows = _block_diag_rows(_head_token_rows(qf[:, 0:512], n_dec, s_dec)).astype(BF16)
        qb_rows = _block_diag_rows(_head_token_rows(qf[:, 512:1024], n_dec, s_dec)).astype(BF16)
        qi_rows = _head_token_rows(qf[:, 1024:1536], n_dec, s_dec).astype(BF16)
        w_col = jnp.pad(sm[:, 64:72].reshape(n_dec, s_dec, 8).transpose(0, 2, 1),
                        ((0, 0), (0, 0), (0, 8 - s_dec))).reshape(n_dec, 64, 1)
        cmk = _features_major_pages(cache_moba_k[l])
        cmv = _features_major_pages(cache_moba_v[l])
        cdk = _features_major_pages(cache_dsa_k[l])
        cdv = _features_major_pages(cache_dsa_v[l])
        cik = _features_major_pages(cache_idx_k[l])
        km_s, scores = _sample_scan(page_table, cmk, cik, qi_rows, w_col)
        nbp = past // MOBA_BLOCK
        ki_new_t = jnp.swapaxes(_pad_new_rows(sm[:, :64], n_dec, s_dec), 1, 2)
        bm, mb = _sample_select(scores, km_s, qa_rows, qi_rows, w_col, ki_new_t, nbp, s_dec, n_sel_s)
        oa_s, ob_s = _sample_attend(
            page_table, cmk, cmv, cdk, cdv, qa_rows, qb_rows, bm, mb, bias_last, bias_new,
            _pad_new_rows(ka, n_dec, s_dec), _pad_new_rows(va, n_dec, s_dec),
            _pad_new_rows(kb, n_dec, s_dec), _pad_new_rows(vb, n_dec, s_dec), s_dec)
        to_t = lambda o: o.reshape(n_dec, 8, 8, 64)[:, :, :s_dec].transpose(1, 3, 0, 2).reshape(512, rows_s)
        xs = _out_ffn(xs, to_t(oa_s), to_t(ob_s), mod_s[2], mod_s[3], mod_s[4], mod_s[5], goa, gob,
                      g_ffn[l][None, :], wo, wg, wu, wd)
        outs_s.append((ka.reshape(n_dec, s_dec, 4, 64), va.reshape(n_dec, s_dec, 4, 64),
                       kb.reshape(n_dec, s_dec, 4, 64), vb.reshape(n_dec, s_dec, 4, 64),
                       sm[:, :64].reshape(n_dec, s_dec, 64)))

    stack = lambda outs, k: jnp.stack([o[k] for o in outs])
    return (xp.reshape(batch, seq, d), xs.reshape(n_dec, s_dec, d),
            *[stack(outs_p, k) for k in range(5)], *[stack(outs_s, k) for k in range(5)])
```

```python
import functools
import math

import numpy as np
import jax
import jax.numpy as jnp
from jax import lax
from jax.experimental import pallas as pl
from jax.experimental.pallas import tpu as pltpu

F32 = jnp.float32
BF16 = jnp.bfloat16

HEAD_DIM = 64
MOBA_BLOCK = 256
MOBA_TOPK = 3
DSA_TOPK = 256
N_BUCKETS = 32
MAX_DISTANCE = 128
EPS = 1e-6
LOG2E = math.log2(math.e)
NEG = -1e30
LOWEST = -3.0e38
ROW_TILE = 512
FFN_TILE = 256
TQ = 256
TK = 256
MOBA_TILES = 8
DSA_TILES = 4
PAGES_PER_STEP = 16
SELECT_BATCH = 8
MAX_BISECT = 256
VMEM_LIMIT = 60 * 1024 * 1024


def _cparams(n_axes):
    return pltpu.CompilerParams(dimension_semantics=("arbitrary",) * n_axes,
                                vmem_limit_bytes=VMEM_LIMIT)


def _resident(block_shape, index_map):
    return pl.BlockSpec(block_shape, index_map, pipeline_mode=pl.Buffered(1))


def _dot(a, b):
    return jnp.dot(a, b, preferred_element_type=F32)


def _dot_nt(a, b):
    return lax.dot_general(a, b, (((1,), (1,)), ((), ())), preferred_element_type=F32)


def _bucket_np(dist):
    n = np.maximum(dist, 0)
    exact = N_BUCKETS // 2
    ratio = np.log(np.maximum(n, 1).astype(np.float32) / np.float32(exact)) / np.float32(
        math.log(MAX_DISTANCE / exact))
    large = np.minimum(exact + (ratio * np.float32(N_BUCKETS - exact)).astype(np.int32), N_BUCKETS - 1)
    return np.where(n < exact, n, large).astype(np.int32)


def _mod_kernel(c_ref, w_ref, b_ref, o_ref):
    c = c_ref[...]
    s = c * jax.nn.sigmoid(c)
    o_ref[...] = jnp.dot(s, w_ref[...], preferred_element_type=F32,
                         precision=lax.Precision.HIGHEST) + b_ref[...]


def _modulation(c_all, w_ada, b_ada):
    rows, d = c_all.shape
    n6 = w_ada.shape[1]
    tn = 1024
    return pl.pallas_call(
        _mod_kernel,
        grid=(n6 // tn,),
        in_specs=[pl.BlockSpec((rows, d), lambda j: (0, 0)),
                  pl.BlockSpec((d, tn), lambda j: (0, j)),
                  pl.BlockSpec((1, tn), lambda j: (0, j))],
        out_specs=pl.BlockSpec((rows, tn), lambda j: (0, j)),
        out_shape=jax.ShapeDtypeStruct((rows, n6), F32),
        compiler_params=_cparams(1),
        name="adaln_modulation",
    )(c_all, w_ada, b_ada)


def _bias_kernel(rb_ref, pat_ref, o_ref):
    h = pl.program_id(0)
    pat = pat_ref[...]
    base = rb_ref[N_BUCKETS - 1, h]
    out = jnp.zeros(pat.shape, F32)
    for b in range(N_BUCKETS - 1):
        out = jnp.where(pat == b, rb_ref[b, h] - base, out)
    o_ref[0] = out * LOG2E


def _bias_tiles(rel_bias, pattern):
    n_heads = rel_bias.shape[1]
    nd = pattern.ndim
    zeros = (0,) * nd
    return pl.pallas_call(
        _bias_kernel,
        grid=(n_heads,),
        in_specs=[pl.BlockSpec(memory_space=pltpu.SMEM),
                  pl.BlockSpec(pattern.shape, lambda h: zeros)],
        out_specs=pl.BlockSpec((1,) + pattern.shape, lambda h: (h,) + zeros),
        out_shape=jax.ShapeDtypeStruct((n_heads,) + pattern.shape, F32),
        compiler_params=_cparams(1),
        name="rel_bias_tiles",
    )(rel_bias, jnp.asarray(pattern))


def _prompt_bias_patterns():
    key = np.arange(TK)[:, None]
    qry = np.arange(TQ)[None, :]
    diag = np.where(qry >= key, _bucket_np(qry - key), N_BUCKETS - 1)
    adj = _bucket_np(TK + qry - key)
    return np.stack([diag, adj]).astype(np.int32)


def _sample_bias_patterns(n_new, page):
    s = np.arange(8)[:, None]
    c = np.arange(page)[None, :]
    last = _bucket_np(page + s - c)
    new = np.where((c <= s) & (c < n_new), _bucket_np(s - c), N_BUCKETS - 1)
    return np.stack([last, new]).astype(np.int32)


def _proj_kernel(x_ref, sh_ref, sc_ref, g_ref, wm_ref, ws_ref, bd_ref, gq_ref, gk_ref,
                 qf_ref, ka_ref, va_ref, kb_ref, vb_ref, sm_ref,
                 kab_ref, kbb_ref, vat_ref, vbt_ref, smb_ref, km_ref, *, wi_scale):
    x = x_ref[...]
    xn = x * lax.rsqrt(jnp.mean(x * x, axis=-1, keepdims=True) + EPS) * g_ref[...]
    h = (xn * (1.0 + sc_ref[...]) + sh_ref[...]).astype(BF16)
    y = _dot(h, wm_ref[...])
    ys = _dot(h, ws_ref[...])

    def head_norm(t, g, width):
        ss = _dot((t * t).astype(BF16), bd_ref[:width, :width])
        return t * lax.rsqrt(ss * (1.0 / HEAD_DIM) + EPS) * g

    scale = HEAD_DIM ** -0.5 * LOG2E
    qa = head_norm(y[:, 0:512], gq_ref[0:1, :], 512)
    ka = head_norm(y[:, 512:768], gk_ref[0:1, :], 256)
    va = y[:, 768:1024]
    qb = head_norm(y[:, 1024:1536], gq_ref[1:2, :], 512)
    kb = head_norm(y[:, 1536:1792], gk_ref[1:2, :], 256)
    vb = y[:, 1792:2048]
    qf_ref[:, 0:512] = qa * scale
    qf_ref[:, 512:1024] = qb * scale
    qf_ref[:, 1024:1536] = y[:, 2048:2560]
    ka_ref[...] = ka
    va_ref[...] = va
    kb_ref[...] = kb
    vb_ref[...] = vb
    lane = lax.broadcasted_iota(jnp.int32, (1, 128), 1)
    sm = ys * jnp.where((lane >= 64) & (lane < 72), wi_scale, 1.0)
    sm_ref[...] = sm
    means = []
    for b in range(ROW_TILE // TK):
        rows = slice(b * TK, (b + 1) * TK)
        kab_ref[b] = ka[rows].astype(BF16)
        kbb_ref[b] = kb[rows].astype(BF16)
        vat_ref[b] = va[rows].T.astype(BF16)
        vbt_ref[b] = vb[rows].T.astype(BF16)
        smb_ref[b] = sm[rows].astype(BF16)
        means.append(jnp.mean(ka[rows], axis=0, keepdims=True))
    km_ref[...] = jnp.concatenate(means, axis=0)


def _project(x, shift, scale, g_attn, w_main, w_small, bd, gq, gk, wi_scale):
    rows, d = x.shape
    nt = rows // ROW_TILE
    nb = rows // TK
    bpt = ROW_TILE // TK
    mrows = shift.shape[0]
    mod_map = (lambda i: (i, 0)) if mrows == rows else (lambda i: (0, 0))
    mod_blk = ROW_TILE if mrows == rows else 1
    row_spec = lambda w: pl.BlockSpec((ROW_TILE, w), lambda i: (i, 0))
    blk_spec = lambda w: pl.BlockSpec((bpt, TK, w), lambda i: (i, 0, 0))
    out_shape = (
        jax.ShapeDtypeStruct((rows, 1536), F32),
        jax.ShapeDtypeStruct((rows, 256), F32),
        jax.ShapeDtypeStruct((rows, 256), F32),
        jax.ShapeDtypeStruct((rows, 256), F32),
        jax.ShapeDtypeStruct((rows, 256), F32),
        jax.ShapeDtypeStruct((rows, 128), F32),
        jax.ShapeDtypeStruct((nb, TK, 256), BF16),
        jax.ShapeDtypeStruct((nb, TK, 256), BF16),
        jax.ShapeDtypeStruct((nb, 256, TK), BF16),
        jax.ShapeDtypeStruct((nb, 256, TK), BF16),
        jax.ShapeDtypeStruct((nb, TK, 128), BF16),
        jax.ShapeDtypeStruct((nt, bpt, 256), F32),
    )
    out_specs = (row_spec(1536), row_spec(256), row_spec(256), row_spec(256), row_spec(256),
                 row_spec(128), blk_spec(256), blk_spec(256), blk_spec(TK), blk_spec(TK),
                 blk_spec(128), pl.BlockSpec((None, bpt, 256), lambda i: (i, 0, 0)))
    in_specs = [row_spec(d),
                pl.BlockSpec((mod_blk, d), mod_map),
                pl.BlockSpec((mod_blk, d), mod_map),
                pl.BlockSpec((1, d), lambda i: (0, 0)),
                _resident(w_main.shape, lambda i: (0, 0)),
                _resident(w_small.shape, lambda i: (0, 0)),
                _resident(bd.shape, lambda i: (0, 0)),
                pl.BlockSpec(gq.shape, lambda i: (0, 0)),
                pl.BlockSpec(gk.shape, lambda i: (0, 0))]
    return pl.pallas_call(
        functools.partial(_proj_kernel, wi_scale=wi_scale),
        grid=(nt,), in_specs=in_specs, out_specs=out_specs, out_shape=out_shape,
        compiler_params=_cparams(1), name="adaln_in_projection",
    )(x, shift, scale, g_attn, w_main, w_small, bd, gq, gk)


def _store_padded_queries(qp_ref, q_ref):
    qT = q_ref[...].T
    zeros = jnp.zeros((64, 2 * TQ), BF16)
    for g in range(4):
        q2 = jnp.concatenate([qT[128 * g:128 * g + 64], qT[128 * g + 64:128 * g + 128]],
                             axis=1).astype(BF16)
        qp_ref[g] = jnp.concatenate([q2 if gg == g else zeros for gg in range(4)], axis=0)


def _for_tile_pairs(n, body, carry, per=2):
    start = 0
    while per >= 1:
        trips = (n - start) // per
        carry = lax.fori_loop(
            0, trips, lambda t, c, s=start, p=per: body([s + p * t + k for k in range(p)], c), carry)
        start = start + per * trips
        per //= 2
    return carry


def _attend(i, qp_ref, k_ref, vt_ref, bias_ref, o_ref, far_masks, diag_masks, stack):
    def tile_update(tiles, carry, tile_masks, near):
        kt = jnp.concatenate([k_ref[j] for j in tiles], axis=0)
        logits = [_dot(kt, qp_ref[g]) for g in range(4)]
        out = []
        for g in range(4):
            m, l, acc = carry[3 * g:3 * g + 3]
            mask = jnp.concatenate([jnp.broadcast_to(tm[g], (TK, 2 * TQ)) for tm in tile_masks], axis=0)
            s = logits[g] + mask
            if near is not None:
                s = s + jnp.concatenate([bias_ref[2 * g, near], bias_ref[2 * g + 1, near]], axis=1)
            m_new = jnp.maximum(m, jnp.max(s, axis=0, keepdims=True))
            if g < 3:
                m_new = jnp.maximum(m_new, logits[g + 1][0:1, :] * 0.0 + NEG)
            alpha = jnp.exp2(m - m_new)
            p = jnp.exp2(s - m_new)
            l = alpha * l + jnp.sum(p, axis=0, keepdims=True)
            vt = jnp.concatenate([vt_ref[j, 64 * g:64 * g + 64, :] for j in tiles], axis=1)
            acc = alpha * acc + _dot(vt, p.astype(BF16))
            out.extend((m_new, l, acc))
        return tuple(out)

    carry = (jnp.full((1, 2 * TQ), NEG, F32), jnp.zeros((1, 2 * TQ), F32),
             jnp.zeros((HEAD_DIM, 2 * TQ), F32)) * 4
    n_far = jnp.maximum(i - 1, 0)
    carry = _for_tile_pairs(n_far, lambda tiles, c: tile_update(tiles, c, [far_masks(j) for j in tiles], None),
                            carry, per=stack)
    carry = lax.fori_loop(n_far, i, lambda j, c: tile_update([j], c, [far_masks(j)], 1), carry)
    carry = tile_update([i], carry, [diag_masks()], 0)
    for g in range(4):
        out = carry[3 * g + 2] / carry[3 * g + 1]
        o_ref[128 * g:128 * g + 64, :] = out[:, :TQ]
        o_ref[128 * g + 64:128 * g + 128, :] = out[:, TQ:]


def _kth_largest(count_ge, counts_init, rmin, rmax, n_adm, n_sel):
    ksel = jnp.float32(n_sel)
    c_top, c_zero, c_pos = counts_init
    few = n_adm <= ksel
    top = ~few & (c_top >= ksel)
    zero = ~few & ~top & (c_pos < ksel) & (c_zero >= ksel)
    pos = c_pos >= ksel
    active = jnp.where(few | top | zero, 0.0, 1.0)
    tau = jnp.where(few, LOWEST, jnp.where(top, rmax, 0.0))
    check = jnp.where(top | zero, 1.0, 0.0)
    lo = jnp.where(pos, 0.0, rmin)
    hi = jnp.where(pos, rmax, 0.0)

    def cond(st):
        return (st[0] < MAX_BISECT) & (jnp.sum(st[1]) > 0.0)

    def step(st):
        it, active, lo, hi, tau, check = st
        mid = lo + (hi - lo) * 0.5
        stuck = (mid <= lo) | (mid >= hi)
        c = count_ge(mid)
        act = active > 0.0
        found = act & ~stuck & (c == ksel)
        ended = act & stuck
        ge = c >= ksel
        tau = jnp.where(found, mid, jnp.where(ended, lo, tau))
        check = jnp.where(ended, 1.0, check)
        lo = jnp.where(ge, mid, lo)
        hi = jnp.where(ge, hi, mid)
        active = jnp.where(found | ended, 0.0, active)
        return it + 1, active, lo, hi, tau, check

    _, active, lo, _, tau, check = lax.while_loop(
        cond, step, (jnp.int32(0), active, lo, hi, tau, check))
    left = active > 0.0
    return jnp.where(left, lo, tau), jnp.where(left, 1.0, check)


def _moba_kernel(q_ref, k_ref, vt_ref, km_ref, bias_ref, o_ref, qp_ref, mc_ref):
    i = pl.program_id(0)
    _store_padded_queries(qp_ref, q_ref)
    blk = lax.broadcasted_iota(jnp.int32, (128, 2 * TQ), 0).astype(F32)
    valid = blk < i.astype(F32)
    kmb = km_ref[...].astype(BF16)
    for g in range(4):
        gate = jnp.where(valid, _dot(kmb, qp_ref[g]), -jnp.inf)
        sel = jnp.zeros_like(gate)
        for _ in range(MOBA_TOPK):
            mx = jnp.max(gate, axis=0, keepdims=True)
            first = jnp.min(jnp.where(gate == mx, blk, 1e9), axis=0, keepdims=True)
            hit = blk == first
            sel = jnp.where(hit & valid, 1.0, sel)
            gate = jnp.where(hit, -jnp.inf, gate)
        mc_ref[g] = jnp.where(sel > 0.0, 0.0, NEG)
    key_io = lax.broadcasted_iota(jnp.int32, (TK, 2 * TQ), 0)
    qry_io = lax.broadcasted_iota(jnp.int32, (TK, 2 * TQ), 1) % TQ
    causal = jnp.where(key_io <= qry_io, 0.0, NEG)
    _attend(i, qp_ref, k_ref, vt_ref, bias_ref, o_ref,
            far_masks=lambda j: [mc_ref[g, pl.ds(j, 1), :] for g in range(4)],
            diag_masks=lambda: [causal] * 4, stack=MOBA_TILES)


def _moba_prompt(qf, kab, vat, kmean, bias):
    nb = kab.shape[0]
    t = nb * TK
    return pl.pallas_call(
        _moba_kernel,
        grid=(t // TQ,),
        in_specs=[pl.BlockSpec((TQ, 512), lambda i: (i, 0)),
                  _resident(kab.shape, lambda i: (0, 0, 0)),
                  _resident(vat.shape, lambda i: (0, 0, 0)),
                  _resident(kmean.shape, lambda i: (0, 0)),
                  _resident((8, 2, TK, TQ), lambda i: (0, 0, 0, 0))],
        out_specs=pl.BlockSpec((512, TQ), lambda i: (0, i)),
        out_shape=jax.ShapeDtypeStruct((512, t), F32),
        scratch_shapes=[pltpu.VMEM((4, 256, 2 * TQ), BF16), pltpu.VMEM((4, 128, 2 * TQ), F32)],
        compiler_params=_cparams(1), name="moba_prompt_attention",
    )(qf, kab, vat, kmean, bias)


def _fold8(x):
    return jnp.sum(x.reshape(TK // 8, 8, TQ), axis=0)


def _dsa_kernel(qb_ref, qi_ref, sm_ref, k_ref, vt_ref, ksm_ref, bias_ref, o_ref, qp_ref, sc_ref,
                *, n_sel):
    i = pl.program_id(0)
    qiT = qi_ref[...].T
    zeros = jnp.zeros((64, TQ), F32)
    qi_all = jnp.concatenate(
        [jnp.concatenate([qiT[64 * h:64 * h + 64], zeros], axis=0) for h in range(8)],
        axis=1).astype(BF16)
    wT = sm_ref[...].T
    w_rows = [wT[64 + h:65 + h, :] for h in range(8)]
    key_io = lax.broadcasted_iota(jnp.int32, (TK, TQ), 0)
    qry_io = lax.broadcasted_iota(jnp.int32, (TK, TQ), 1)
    causal = key_io <= qry_io

    def score_tile(j):
        z = _dot(ksm_ref[j], qi_all)
        sc = w_rows[0] * jnp.maximum(z[:, 0:TQ], 0.0)
        for h in range(1, 8):
            sc = sc + w_rows[h] * jnp.maximum(z[:, TQ * h:TQ * (h + 1)], 0.0)
        return sc

    def fold_max(x):
        return jnp.max(x.reshape(TK // 8, 8, TQ), axis=0)

    def stats(carry, hi, lo):
        mx, mn, c_zero, c_pos = carry
        return (jnp.maximum(mx, fold_max(hi)), jnp.minimum(mn, -fold_max(-lo)),
                c_zero + _fold8(jnp.where(hi >= 0.0, 1.0, 0.0)),
                c_pos + _fold8(jnp.where(hi > 0.0, 1.0, 0.0)))

    def fill(tiles, carry):
        for j in tiles:
            sc = score_tile(j)
            sc_ref[j] = sc
            carry = stats(carry, sc, sc)
        return carry

    carry = _for_tile_pairs(i, fill, (jnp.full((8, TQ), -jnp.inf, F32), jnp.full((8, TQ), jnp.inf, F32),
                                      jnp.zeros((8, TQ), F32), jnp.zeros((8, TQ), F32)), per=4)
    scd = score_tile(i)
    diag = jnp.where(causal, scd, -jnp.inf)
    sc_ref[i] = diag
    mx, mn, c_zero, c_pos = stats(carry, diag, jnp.where(causal, scd, jnp.inf))
    rmax = jnp.max(mx, axis=0, keepdims=True)
    rmin = jnp.min(mn, axis=0, keepdims=True)

    def count(pred):
        def body(tiles, acc):
            for j in tiles:
                acc = acc + _fold8(jnp.where(pred(sc_ref[j]), 1.0, 0.0))
            return acc
        return jnp.sum(_for_tile_pairs(i + 1, body, jnp.zeros((8, TQ), F32)), axis=0, keepdims=True)

    init = (count(lambda x: x >= rmax), jnp.sum(c_zero, axis=0, keepdims=True),
            jnp.sum(c_pos, axis=0, keepdims=True))
    n_adm = (i * TQ + 1).astype(F32) + lax.broadcasted_iota(jnp.int32, (1, TQ), 1).astype(F32)
    tau, check = _kth_largest(lambda t: count(lambda x: x >= t), init, rmin, rmax, n_adm, n_sel)

    @pl.when(jnp.sum(check) > 0.0)
    def _():
        need = jnp.float32(n_sel) - count(lambda x: x > tau)
        lower = jnp.where(lax.broadcasted_iota(jnp.int32, (TK, TK), 0)
                          >= lax.broadcasted_iota(jnp.int32, (TK, TK), 1), 1.0, 0.0).astype(BF16)

        def fix(j, seen):
            t = sc_ref[j]
            eq = jnp.where((t == tau) & (check > 0.0), 1.0, 0.0)
            rank = _dot(lower, eq.astype(BF16)) + seen
            sc_ref[j] = jnp.where((eq > 0.0) & (rank > need), -jnp.inf, t)
            return seen + jnp.sum(_fold8(eq), axis=0, keepdims=True)

        lax.fori_loop(0, i + 1, fix, jnp.zeros((1, TQ), F32))

    _store_padded_queries(qp_ref, qb_ref)

    def sel_masks(j):
        mb = jnp.where(sc_ref[j] >= tau, 0.0, NEG)
        return [jnp.concatenate([mb, mb], axis=1)] * 4

    _attend(i, qp_ref, k_ref, vt_ref, bias_ref, o_ref,
            far_masks=sel_masks, diag_masks=lambda: sel_masks(i), stack=DSA_TILES)


def _dsa_prompt(qf, sm, kbb, vbt, smb, bias, n_sel):
    nb = kbb.shape[0]
    t = nb * TK
    return pl.pallas_call(
        functools.partial(_dsa_kernel, n_sel=n_sel),
        grid=(t // TQ,),
        in_specs=[pl.BlockSpec((TQ, 512), lambda i: (i, 1)),
                  pl.BlockSpec((TQ, 512), lambda i: (i, 2)),
                  pl.BlockSpec((TQ, 128), lambda i: (i, 0)),
                  _resident(kbb.shape, lambda i: (0, 0, 0)),
                  _resident(vbt.shape, lambda i: (0, 0, 0)),
                  _resident(smb.shape, lambda i: (0, 0, 0)),
                  _resident((8, 2, TK, TQ), lambda i: (1, 0, 0, 0))],
        out_specs=pl.BlockSpec((512, TQ), lambda i: (0, i)),
        out_shape=jax.ShapeDtypeStruct((512, t), F32),
        scratch_shapes=[pltpu.VMEM((4, 256, 2 * TQ), BF16), pltpu.VMEM((nb, TK, TQ), F32)],
        compiler_params=_cparams(1), name="dsa_prompt_attention",
    )(qf, qf, sm, kbb, vbt, smb, bias)


def _ffn_kernel(x_ref, oa_ref, ob_ref, gt1_ref, sh2_ref, sc2_ref, gt2_ref, goa_ref, gob_ref, gf_ref,
                wo_ref, wg_ref, wu_ref, wd_ref, y_ref):
    def feat_norm(o, g):
        return o * lax.rsqrt(jnp.mean(o * o, axis=0, keepdims=True) + EPS) * g

    cat = jnp.concatenate([feat_norm(oa_ref[...], goa_ref[...]),
                           feat_norm(ob_ref[...], gob_ref[...])], axis=0)
    att = _dot(cat.T.astype(BF16), wo_ref[...])
    x1 = x_ref[...] + gt1_ref[...] * att
    xn = x1 * lax.rsqrt(jnp.mean(x1 * x1, axis=-1, keepdims=True) + EPS) * gf_ref[...]
    h = (xn * (1.0 + sc2_ref[...]) + sh2_ref[...]).astype(BF16)
    gate = _dot(h, wg_ref[...])
    up = _dot(h, wu_ref[...])
    act = (gate * jax.nn.sigmoid(gate) * up).astype(BF16)
    y_ref[...] = x1 + gt2_ref[...] * _dot(act, wd_ref[...])


def _out_ffn(x, oat, obt, gt1, sh2, sc2, gt2, goa, gob, g_ffn, wo, wg, wu, wd):
    rows, d = x.shape
    mrows = gt1.shape[0]
    mod_map = (lambda i: (i, 0)) if mrows == rows else (lambda i: (0, 0))
    mod_blk = FFN_TILE if mrows == rows else 1
    mod_spec = pl.BlockSpec((mod_blk, d), mod_map)
    col_spec = pl.BlockSpec((512, FFN_TILE), lambda i: (0, i))
    return pl.pallas_call(
        _ffn_kernel,
        grid=(rows // FFN_TILE,),
        in_specs=[pl.BlockSpec((FFN_TILE, d), lambda i: (i, 0)), col_spec, col_spec,
                  mod_spec, mod_spec, mod_spec, mod_spec,
                  pl.BlockSpec((512, 1), lambda i: (0, 0)),
                  pl.BlockSpec((512, 1), lambda i: (0, 0)),
                  pl.BlockSpec((1, d), lambda i: (0, 0)),
                  _resident(wo.shape, lambda i: (0, 0)),
                  _resident(wg.shape, lambda i: (0, 0)),
                  _resident(wu.shape, lambda i: (0, 0)),
                  _resident(wd.shape, lambda i: (0, 0))],
        out_specs=pl.BlockSpec((FFN_TILE, d), lambda i: (i, 0)),
        out_shape=jax.ShapeDtypeStruct((rows, d), F32),
        compiler_params=_cparams(1), name="out_projection_ffn",
    )(x, oat, obt, gt1, sh2, sc2, gt2, goa, gob, g_ffn, wo, wg, wu, wd)


def _stream_pages(pt_ref, hbm_refs, buf_refs, sem_ref, ppc, nc, chunk_body):
    n = pl.program_id(0)
    n_last = pl.num_programs(0) - 1

    def copy(a, page, slot, p):
        return pltpu.make_async_copy(hbm_refs[a].at[page], buf_refs[a].at[slot, p], sem_ref.at[slot, a])

    def start(batch, c, slot):
        for a in range(len(hbm_refs)):
            for p in range(ppc):
                copy(a, pt_ref[batch, c * ppc + p], slot, p).start()

    def wait(slot):
        for a in range(len(hbm_refs)):
            for p in range(ppc):
                copy(a, 0, slot, p).wait()

    @pl.when(n == 0)
    def _():
        start(0, 0, 0)

    for c in range(nc):
        slot = c % 2
        if c + 1 < nc:
            start(n, c + 1, 1 - slot)
        else:
            @pl.when(n < n_last)
            def _():
                start(n + 1, 0, 0)
        wait(slot)
        chunk_body(c, [buf.at[slot] for buf in buf_refs])


def _head_sum(z):
    out = z[0:8]
    for h in range(1, 8):
        out = out + z[8 * h:8 * h + 8]
    return out


def _sample_scan_kernel(pt_ref, mk_hbm, ik_hbm, qi_ref, w_ref, km_ref, sc_ref, mk_buf, ik_buf, sem,
                        *, ppc, nc):
    ppb = MOBA_BLOCK // 128
    bpc = ppc // ppb
    lane = lax.broadcasted_iota(jnp.int32, (256, 128), 1)
    qi = qi_ref[...]
    w = w_ref[...]
    km_ref[...] = jnp.zeros(km_ref.shape, F32)

    def chunk(c, bufs):
        mk, ik = bufs
        km = km_ref[...]
        for b in range(bpc):
            ksum = mk[ppb * b]
            for p in range(1, ppb):
                ksum = ksum + mk[ppb * b + p]
            kmean = jnp.sum(ksum, axis=1, keepdims=True) * (1.0 / MOBA_BLOCK)
            km = km + jnp.where(lane == c * bpc + b, kmean, 0.0)
        km_ref[...] = km
        for p in range(ppc):
            z = _dot(qi, ik[p].astype(BF16))
            col = (c * ppc + p) * 128
            sc_ref[:, col:col + 128] = _head_sum(jnp.maximum(z, 0.0) * w)

    _stream_pages(pt_ref, [mk_hbm, ik_hbm], [mk_buf, ik_buf], sem, ppc, nc, chunk)


def _sample_scan(page_table, cache_mk, cache_ik, qi_rows, w_col):
    n, n_pages = page_table.shape
    ppc = PAGES_PER_STEP
    nc = n_pages // ppc
    any_spec = pl.BlockSpec(memory_space=pl.ANY)
    grid_spec = pltpu.PrefetchScalarGridSpec(
        num_scalar_prefetch=1, grid=(n,),
        in_specs=[any_spec, any_spec,
                  pl.BlockSpec((None, 64, 64), lambda b, pt: (b, 0, 0)),
                  pl.BlockSpec((None, 64, 1), lambda b, pt: (b, 0, 0))],
        out_specs=(pl.BlockSpec((None, 256, 128), lambda b, pt: (b, 0, 0)),
                   pl.BlockSpec((None, 8, n_pages * 128), lambda b, pt: (b, 0, 0))),
        scratch_shapes=[pltpu.VMEM((2, ppc, 256, 128), F32), pltpu.VMEM((2, ppc, 64, 128), F32),
                        pltpu.SemaphoreType.DMA((2, 2))])
    return pl.pallas_call(
        functools.partial(_sample_scan_kernel, ppc=ppc, nc=nc),
        grid_spec=grid_spec,
        out_shape=(jax.ShapeDtypeStruct((n, 256, 128), F32),
                   jax.ShapeDtypeStruct((n, 8, n_pages * 128), F32)),
        compiler_params=_cparams(1), name="sample_page_scan",
    )(page_table, cache_mk, cache_ik, qi_rows, w_col)


def _sample_select_kernel(sc_ref, km_ref, qa_ref, qi_ref, w_ref, kin_ref, bm_ref, mb_ref,
                          *, n_blocks, n_new, n_sel, idx_iter):
    nb = SELECT_BATCH
    rows = nb * 64
    lane = lax.broadcasted_iota(jnp.int32, (rows, 128), 1).astype(F32)
    valid = lane < float(n_blocks)
    gate = jnp.concatenate([_dot(qa_ref[b], km_ref[b].astype(BF16)) for b in range(nb)], axis=0)
    gate = jnp.where(valid, gate, -jnp.inf)
    sel = jnp.zeros_like(gate)
    for _ in range(min(MOBA_TOPK, n_blocks)):
        mx = jnp.max(gate, axis=1, keepdims=True)
        first = jnp.min(jnp.where(gate == mx, lane, 1e9), axis=1, keepdims=True)
        hit = lane == first
        sel = jnp.where(hit & valid, 1.0, sel)
        gate = jnp.where(hit, -jnp.inf, gate)
    bm_ref[...] = jnp.where(sel > 0.0, 0.0, NEG).reshape(nb, 64, 128)

    new = jnp.concatenate(
        [_head_sum(jnp.maximum(_dot(qi_ref[b], kin_ref[b]), 0.0) * w_ref[b]) for b in range(nb)], axis=0)
    tok = lax.broadcasted_iota(jnp.int32, (nb * 8, 128), 0) % 8
    col = lax.broadcasted_iota(jnp.int32, (nb * 8, 128), 1)
    new = jnp.where((col <= tok) & (col < n_new), new, -jnp.inf)
    past = sc_ref.shape[-1]
    a = jnp.concatenate([sc_ref[...].reshape(nb * 8, past), new], axis=1)
    width = past + 128
    idx = lax.broadcasted_iota(jnp.int32, (nb * 8, width), 1).astype(F32)
    adm = a > -jnp.inf

    def count(mask):
        return jnp.sum(jnp.where(mask, 1.0, 0.0), axis=1, keepdims=True)

    rmax = jnp.max(a, axis=1, keepdims=True)
    rmin = jnp.min(jnp.where(adm, a, jnp.inf), axis=1, keepdims=True)
    init = (count(a >= rmax), count(a >= 0.0), count(a > 0.0))
    tau, check = _kth_largest(lambda t: count(a >= t), init, rmin, rmax, count(adm), n_sel)

    eq = a == tau
    need = jnp.float32(n_sel) - count(a > tau)

    def bis(_, carry):
        lj, hj = carry
        mj = jnp.floor((lj + hj) * 0.5)
        ok = count(eq & (idx <= mj)) >= need
        return jnp.where(ok, lj, mj), jnp.where(ok, mj, hj)

    n_bis = jnp.where(jnp.sum(check) > 0.0, idx_iter, 0)
    _, hj = lax.fori_loop(0, n_bis, bis, (jnp.full((nb * 8, 1), -1.0, F32),
                                          jnp.full((nb * 8, 1), float(width - 1), F32)))
    hj = jnp.where(check > 0.0, hj, float(width))
    keep = (a > tau) | (eq & (idx <= hj))
    mb_ref[...] = jnp.where(keep, 0.0, NEG).reshape(nb, 8, width)


def _sample_select(scores, kmean_t, qa_rows, qi_rows, w_col, ki_new_t, n_blocks, n_new, n_sel):
    n, _, past = scores.shape
    nb = SELECT_BATCH
    width = past + 128
    kern = functools.partial(_sample_select_kernel, n_blocks=n_blocks, n_new=n_new, n_sel=n_sel,
                             idx_iter=int(math.ceil(math.log2(width))) + 1)
    per_n = lambda *shape: pl.BlockSpec((nb,) + shape, lambda b: (b,) + (0,) * len(shape))
    return pl.pallas_call(
        kern, grid=(n // nb,),
        in_specs=[per_n(8, past), per_n(256, 128), per_n(64, 256), per_n(64, 64), per_n(64, 1),
                  per_n(64, 128)],
        out_specs=(per_n(64, 128), per_n(8, width)),
        out_shape=(jax.ShapeDtypeStruct((n, 64, 128), F32),
                   jax.ShapeDtypeStruct((n, 8, width), F32)),
        compiler_params=_cparams(1), name="sample_selection",
    )(scores, kmean_t, qa_rows, qi_rows, w_col, ki_new_t)


def _row_update(s, pv, state):
    m, l, acc = state
    m_new = jnp.maximum(m, jnp.max(s, axis=1, keepdims=True))
    alpha = jnp.exp2(m - m_new)
    p = jnp.exp2(s - m_new)
    return (m_new, alpha * l + jnp.sum(p, axis=1, keepdims=True), alpha * acc + pv(p.astype(BF16)))


def _sample_attn_kernel(pt_ref, mk_hbm, mv_hbm, dk_hbm, dv_hbm, qa_ref, qb_ref, bm_ref, mb_ref, bl_ref,
                        bn_ref, kan_ref, van_ref, kbn_ref, vbn_ref, oa_ref, ob_ref,
                        mk_buf, mv_buf, dk_buf, dv_buf, sem, *, ppc, nc, n_new):
    ppb = MOBA_BLOCK // 128
    bpc = ppc // ppb
    lane = lax.broadcasted_iota(jnp.int32, (64, 128), 1)
    tile8 = lambda x: jnp.concatenate([x] * 8, axis=0)
    qa = qa_ref[...]
    qb = qb_ref[...]
    bm = bm_ref[...]
    fresh = (jnp.full((64, 1), NEG, F32), jnp.zeros((64, 1), F32), jnp.zeros((64, 256), F32))
    states = [fresh, fresh]

    def logits(q, pages):
        return jnp.concatenate([_dot(q, pages[p].astype(BF16)) for p in range(ppc)], axis=1)

    def values(pages):
        def pv(p):
            out = _dot_nt(p[:, 0:128], pages[0].astype(BF16))
            for k in range(1, ppc):
                out = out + _dot_nt(p[:, 128 * k:128 * (k + 1)], pages[k].astype(BF16))
            return out
        return pv

    def chunk(c, bufs):
        mk, mv, dk, dv = bufs
        gate = jnp.concatenate(
            [jnp.broadcast_to(jnp.sum(jnp.where(lane == c * bpc + b, bm, 0.0), axis=1, keepdims=True),
                              (64, MOBA_BLOCK)) for b in range(bpc)], axis=1)
        sa = logits(qa, mk) + gate
        sb = logits(qb, dk) + tile8(mb_ref[:, c * ppc * 128:(c + 1) * ppc * 128])
        if c == nc - 1:
            pad = jnp.zeros((64, (ppc - 1) * 128), F32)
            sa = sa + jnp.concatenate([pad, bl_ref[0]], axis=1)
            sb = sb + jnp.concatenate([pad, bl_ref[1]], axis=1)
        states[0] = _row_update(sa, values(mv), states[0])
        states[1] = _row_update(sb, values(dv), states[1])

    _stream_pages(pt_ref, [mk_hbm, mv_hbm, dk_hbm, dv_hbm], [mk_buf, mv_buf, dk_buf, dv_buf], sem,
                  ppc, nc, chunk)

    tok = lax.broadcasted_iota(jnp.int32, (64, 128), 0) % 8
    own = (lane <= tok) & (lane < n_new)
    s = jnp.where(own, _dot_nt(qa, kan_ref[...]) + bn_ref[0], NEG)
    states[0] = _row_update(s, lambda p: _dot(p, van_ref[...]), states[0])
    s = _dot_nt(qb, kbn_ref[...]) + bn_ref[1] + tile8(mb_ref[:, nc * ppc * 128:nc * ppc * 128 + 128])
    states[1] = _row_update(s, lambda p: _dot(p, vbn_ref[...]), states[1])
    grp = lax.broadcasted_iota(jnp.int32, (64, 64), 0) // 16
    for o_ref, (m, l, acc) in ((oa_ref, states[0]), (ob_ref, states[1])):
        o = acc / l
        out = jnp.zeros((64, 64), F32)
        for g in range(4):
            out = out + jnp.where(grp == g, o[:, 64 * g:64 * g + 64], 0.0)
        o_ref[...] = out


def _sample_attend(page_table, cache_mk, cache_mv, cache_dk, cache_dv, qa_rows, qb_rows, bm, mb,
                   bias_last, bias_new, ka_new, va_new, kb_new, vb_new, n_new):
    n, n_pages = page_table.shape
    ppc = PAGES_PER_STEP
    nc = n_pages // ppc
    width = mb.shape[-1]
    any_spec = pl.BlockSpec(memory_space=pl.ANY)
    per_n = lambda *shape: pl.BlockSpec((None,) + shape, lambda b, pt: (b,) + (0,) * len(shape))
    const = lambda *shape: pl.BlockSpec(shape, lambda b, pt: (0,) * len(shape))
    in_specs = [any_spec] * 4 + [
        per_n(64, 256), per_n(64, 256), per_n(64, 128), per_n(8, width),
        const(2, 64, 128), const(2, 64, 128),
        per_n(128, 256), per_n(128, 256), per_n(128, 256), per_n(128, 256)]
    grid_spec = pltpu.PrefetchScalarGridSpec(
        num_scalar_prefetch=1, grid=(n,), in_specs=in_specs,
        out_specs=(per_n(64, 64), per_n(64, 64)),
        scratch_shapes=[pltpu.VMEM((2, ppc, 256, 128), F32)] * 4 + [pltpu.SemaphoreType.DMA((2, 4))])
    return pl.pallas_call(
        functools.partial(_sample_attn_kernel, ppc=ppc, nc=nc, n_new=n_new),
        grid_spec=grid_spec,
        out_shape=(jax.ShapeDtypeStruct((n, 64, 64), F32), jax.ShapeDtypeStruct((n, 64, 64), F32)),
        compiler_params=_cparams(1), name="sample_paged_attention",
    )(page_table, cache_mk, cache_mv, cache_dk, cache_dv,
      qa_rows, qb_rows, bm, mb, bias_last, bias_new, ka_new, va_new, kb_new, vb_new)


def _head_token_rows(q, n, s):
    q = q.reshape(n, s, 8, HEAD_DIM).transpose(0, 2, 1, 3)
    q = jnp.pad(q, ((0, 0), (0, 0), (0, 8 - s), (0, 0)))
    return q.reshape(n, 64, HEAD_DIM)


def _block_diag_rows(q_rows):
    n = q_rows.shape[0]
    grp = jnp.arange(64) // 16
    onehot = (grp[:, None] == jnp.arange(4)[None, :]).astype(q_rows.dtype)
    return (q_rows[:, :, None, :] * onehot[None, :, :, None]).reshape(n, 64, 256)


def _pad_new_rows(x, n, s):
    w = x.shape[1]
    return jnp.pad(x.reshape(n, s, w), ((0, 0), (0, 128 - s), (0, 0))).astype(BF16)


def _features_major_pages(cache):
    n_pool, page = cache.shape[:2]
    return jnp.moveaxis(cache.reshape(n_pool, page, -1), 1, 2)


def kernel(x_prompt, x_sample, cache_moba_k, cache_moba_v, cache_dsa_k, cache_dsa_v, cache_idx_k,
           page_table, c_prompt, c_sample, rel_bias, g_attn, g_ffn, w_ada, b_ada, w_in, g_qa, g_ka,
           g_qb, g_kb, g_oa, g_ob, w_out, w_gate, w_up, w_down):
    batch, seq, d = x_prompt.shape
    n_dec, s_dec, _ = x_sample.shape
    depth = w_in.shape[0]
    page = cache_idx_k.shape[2]
    n_pages = page_table.shape[1]
    past = n_pages * page
    assert batch == 1 and d == 1024 and rel_bias.shape == (N_BUCKETS, 16)
    assert seq % ROW_TILE == 0 and (n_dec * s_dec) % ROW_TILE == 0 and s_dec <= 8
    assert page == 128 and past % MOBA_BLOCK == 0 and n_pages % (2 * PAGES_PER_STEP) == 0
    assert seq // MOBA_BLOCK <= 128 and past // MOBA_BLOCK <= 128 and n_dec % SELECT_BATCH == 0
    rows_s = n_dec * s_dec
    n_sel_p = min(DSA_TOPK, seq // 4)
    n_sel_s = min(DSA_TOPK, (past + s_dec) // 4)
    idx_width = 512
    wi_scale = float(idx_width) ** -0.5

    bias_p = _bias_tiles(rel_bias, _prompt_bias_patterns())
    bias_s = _bias_tiles(rel_bias, _sample_bias_patterns(s_dec, page))
    bias_last = bias_s[:, 0].reshape(2, 64, 128)
    bias_new = bias_s[:, 1].reshape(2, 64, 128)
    bd = jnp.asarray(np.kron(np.eye(8, dtype=np.float32), np.ones((64, 64), np.float32)), BF16)

    xp = x_prompt.reshape(seq, d)
    xs = x_sample.reshape(rows_s, d)
    pad_rows = (-(1 + n_dec)) % 8
    c_all = jnp.concatenate([c_prompt, c_sample, jnp.zeros((pad_rows, d), F32)], axis=0)
    outs_p, outs_s = [], []
    for l in range(depth):
        mod = _modulation(c_all, w_ada[l], b_ada[l][None, :])
        mod_p = [mod[0:1, k * d:(k + 1) * d] for k in range(6)]
        mod_s = [jnp.repeat(mod[1:1 + n_dec, k * d:(k + 1) * d], s_dec, axis=0) for k in range(6)]
        w_main = w_in[l][:, 0:2560].astype(BF16)
        w_small = jnp.pad(w_in[l][:, 2560:2632], ((0, 0), (0, 56))).astype(BF16)
        gq = jnp.stack([jnp.tile(g_qa[l], 8), jnp.tile(g_qb[l], 8)])
        gk = jnp.stack([jnp.tile(g_ka[l], 4), jnp.tile(g_kb[l], 4)])
        goa, gob = g_oa[l][:, None], g_ob[l][:, None]
        wo, wg, wu, wd = (w_out[l].astype(BF16), w_gate[l].astype(BF16), w_up[l].astype(BF16),
                          w_down[l].astype(BF16))

        (qf, ka, va, kb, vb, sm, kab, kbb, vat, vbt, smb, km) = _project(
            xp, mod_p[0], mod_p[1], g_attn[l][None, :], w_main, w_small, bd, gq, gk, wi_scale)
        nb = seq // MOBA_BLOCK
        kmean = jnp.pad(km.reshape(nb, 256), ((0, 128 - nb), (0, 0)))
        oat = _moba_prompt(qf, kab, vat, kmean, bias_p)
        obt = _dsa_prompt(qf, sm, kbb, vbt, smb, bias_p, n_sel_p)
        xp = _out_ffn(xp, oat, obt, mod_p[2], mod_p[3], mod_p[4], mod_p[5], goa, gob,
                      g_ffn[l][None, :], wo, wg, wu, wd)
        outs_p.append((ka.reshape(1, seq, 4, 64), va.reshape(1, seq, 4, 64), kb.reshape(1, seq, 4, 64),
                       vb.reshape(1, seq, 4, 64), sm[:, :64].reshape(1, seq, 64)))

        (qf, ka, va, kb, vb, sm, _, _, _, _, _, _) = _project(
            xs, mod_s[0], mod_s[1], g_attn[l][None, :], w_main, w_small, bd, gq, gk, wi_scale)
        qa_rows = _block_diag_rows(_head_token_rows(qf[:, 0:512], n_dec, s_dec)).astype(BF16)
        qb_rows = _block_diag_rows(_head_token_rows(qf[:, 512:1024], n_dec, s_dec)).astype(BF16)
        qi_rows = _head_token_rows(qf[:, 1024:1536], n_dec, s_dec).astype(BF16)
        w_col = jnp.pad(sm[:, 64:72].reshape(n_dec, s_dec, 8).transpose(0, 2, 1),
                        ((0, 0), (0, 0), (0, 8 - s_dec))).reshape(n_dec, 64, 1)
        cmk = _features_major_pages(cache_moba_k[l])
        cmv = _features_major_pages(cache_moba_v[l])
        cdk = _features_major_pages(cache_dsa_k[l])
        cdv = _features_major_pages(cache_dsa_v[l])
        cik = _features_major_pages(cache_idx_k[l])
        km_s, scores = _sample_scan(page_table, cmk, cik, qi_rows, w_col)
        nbp = past // MOBA_BLOCK
        ki_new_t = jnp.swapaxes(_pad_new_rows(sm[:, :64], n_dec, s_dec), 1, 2)
        bm, mb = _sample_select(scores, km_s, qa_rows, qi_rows, w_col, ki_new_t, nbp, s_dec, n_sel_s)
        oa_s, ob_s = _sample_attend(
            page_table, cmk, cmv, cdk, cdv, qa_rows, qb_rows, bm, mb, bias_last, bias_new,
            _pad_new_rows(ka, n_dec, s_dec), _pad_new_rows(va, n_dec, s_dec),
            _pad_new_rows(kb, n_dec, s_dec), _pad_new_rows(vb, n_dec, s_dec), s_dec)
        to_t = lambda o: o.reshape(n_dec, 8, 8, 64)[:, :, :s_dec].transpose(1, 3, 0, 2).reshape(512, rows_s)
        xs = _out_ffn(xs, to_t(oa_s), to_t(ob_s), mod_s[2], mod_s[3], mod_s[4], mod_s[5], goa, gob,
                      g_ffn[l][None, :], wo, wg, wu, wd)
        outs_s.append((ka.reshape(n_dec, s_dec, 4, 64), va.reshape(n_dec, s_dec, 4, 64),
                       kb.reshape(n_dec, s_dec, 4, 64), vb.reshape(n_dec, s_dec, 4, 64),
                       sm[:, :64].reshape(n_dec, s_dec, 64)))

    stack = lambda outs, k: jnp.stack([o[k] for o in outs])
    return (xp.reshape(batch, seq, d), xs.reshape(n_dec, s_dec, d),
            *[stack(outs_p, k) for k in range(5)], *[stack(outs_s, k) for k in range(5)])
```

```python
import functools
import math

import numpy as np
import jax
import jax.numpy as jnp
from jax import lax
from jax.experimental import pallas as pl
from jax.experimental.pallas import tpu as pltpu

F32 = jnp.float32
BF16 = jnp.bfloat16

HEAD_DIM = 64
MOBA_BLOCK = 256
MOBA_TOPK = 3
DSA_TOPK = 256
N_BUCKETS = 32
MAX_DISTANCE = 128
EPS = 1e-6
LOG2E = math.log2(math.e)
NEG = -1e30
LOWEST = -3.0e38
ROW_TILE = 512
FFN_TILE = 256
TQ = 256
TK = 256
MOBA_TILES = 8
DSA_TILES = 4
PAGES_PER_STEP = 16
SELECT_BATCH = 8
MAX_BISECT = 256
VMEM_LIMIT = 60 * 1024 * 1024


def _cparams(n_axes):
    return pltpu.CompilerParams(dimension_semantics=("arbitrary",) * n_axes,
                                vmem_limit_bytes=VMEM_LIMIT)


def _resident(block_shape, index_map):
    return pl.BlockSpec(block_shape, index_map, pipeline_mode=pl.Buffered(1))


def _dot(a, b):
    return jnp.dot(a, b, preferred_element_type=F32)


def _dot_nt(a, b):
    return lax.dot_general(a, b, (((1,), (1,)), ((), ())), preferred_element_type=F32)


def _bucket_np(dist):
    n = np.maximum(dist, 0)
    exact = N_BUCKETS // 2
    ratio = np.log(np.maximum(n, 1).astype(np.float32) / np.float32(exact)) / np.float32(
        math.log(MAX_DISTANCE / exact))
    large = np.minimum(exact + (ratio * np.float32(N_BUCKETS - exact)).astype(np.int32), N_BUCKETS - 1)
    return np.where(n < exact, n, large).astype(np.int32)


def _mod_kernel(c_ref, w_ref, b_ref, o_ref):
    c = c_ref[...]
    s = c * jax.nn.sigmoid(c)
    o_ref[...] = jnp.dot(s, w_ref[...], preferred_element_type=F32,
                         precision=lax.Precision.HIGHEST) + b_ref[...]


def _modulation(c_all, w_ada, b_ada):
    rows, d = c_all.shape
    n6 = w_ada.shape[1]
    tn = 1024
    return pl.pallas_call(
        _mod_kernel,
        grid=(n6 // tn,),
        in_specs=[pl.BlockSpec((rows, d), lambda j: (0, 0)),
                  pl.BlockSpec((d, tn), lambda j: (0, j)),
                  pl.BlockSpec((1, tn), lambda j: (0, j))],
        out_specs=pl.BlockSpec((rows, tn), lambda j: (0, j)),
        out_shape=jax.ShapeDtypeStruct((rows, n6), F32),
        compiler_params=_cparams(1),
        name="adaln_modulation",
    )(c_all, w_ada, b_ada)


def _bias_kernel(rb_ref, pat_ref, o_ref):
    h = pl.program_id(0)
    pat = pat_ref[...]
    base = rb_ref[N_BUCKETS - 1, h]
    out = jnp.zeros(pat.shape, F32)
    for b in range(N_BUCKETS - 1):
        out = jnp.where(pat == b, rb_ref[b, h] - base, out)
    o_ref[0] = out * LOG2E


def _bias_tiles(rel_bias, pattern):
    n_heads = rel_bias.shape[1]
    nd = pattern.ndim
    zeros = (0,) * nd
    return pl.pallas_call(
        _bias_kernel,
        grid=(n_heads,),
        in_specs=[pl.BlockSpec(memory_space=pltpu.SMEM),
                  pl.BlockSpec(pattern.shape, lambda h: zeros)],
        out_specs=pl.BlockSpec((1,) + pattern.shape, lambda h: (h,) + zeros),
        out_shape=jax.ShapeDtypeStruct((n_heads,) + pattern.shape, F32),
        compiler_params=_cparams(1),
        name="rel_bias_tiles",
    )(rel_bias, jnp.asarray(pattern))


def _prompt_bias_patterns():
    key = np.arange(TK)[:, None]
    qry = np.arange(TQ)[None, :]
    diag = np.where(qry >= key, _bucket_np(qry - key), N_BUCKETS - 1)
    adj = _bucket_np(TK + qry - key)
    return np.stack([diag, adj]).astype(np.int32)


def _sample_bias_patterns(n_new, page):
    s = np.arange(8)[:, None]
    c = np.arange(page)[None, :]
    last = _bucket_np(page + s - c)
    new = np.where((c <= s) & (c < n_new), _bucket_np(s - c), N_BUCKETS - 1)
    return np.stack([last, new]).astype(np.int32)


def _proj_kernel(x_ref, sh_ref, sc_ref, g_ref, wm_ref, ws_ref, bd_ref, gq_ref, gk_ref,
                 qf_ref, ka_ref, va_ref, kb_ref, vb_ref, sm_ref,
                 kab_ref, kbb_ref, vat_ref, vbt_ref, smb_ref, km_ref, *, wi_scale):
    x = x_ref[...]
    xn = x * lax.rsqrt(jnp.mean(x * x, axis=-1, keepdims=True) + EPS) * g_ref[...]
    h = (xn * (1.0 + sc_ref[...]) + sh_ref[...]).astype(BF16)
    y = _dot(h, wm_ref[...])
    ys = _dot(h, ws_ref[...])

    def head_norm(t, g, width):
        ss = _dot((t * t).astype(BF16), bd_ref[:width, :width])
        return t * lax.rsqrt(ss * (1.0 / HEAD_DIM) + EPS) * g

    scale = HEAD_DIM ** -0.5 * LOG2E
    qa = head_norm(y[:, 0:512], gq_ref[0:1, :], 512)
    ka = head_norm(y[:, 512:768], gk_ref[0:1, :], 256)
    va = y[:, 768:1024]
    qb = head_norm(y[:, 1024:1536], gq_ref[1:2, :], 512)
    kb = head_norm(y[:, 1536:1792], gk_ref[1:2, :], 256)
    vb = y[:, 1792:2048]
    qf_ref[:, 0:512] = qa * scale
    qf_ref[:, 512:1024] = qb * scale
    qf_ref[:, 1024:1536] = y[:, 2048:2560]
    ka_ref[...] = ka
    va_ref[...] = va
    kb_ref[...] = kb
    vb_ref[...] = vb
    lane = lax.broadcasted_iota(jnp.int32, (1, 128), 1)
    sm = ys * jnp.where((lane >= 64) & (lane < 72), wi_scale, 1.0)
    sm_ref[...] = sm
    means = []
    for b in range(ROW_TILE // TK):
        rows = slice(b * TK, (b + 1) * TK)
        kab_ref[b] = ka[rows].astype(BF16)
        kbb_ref[b] = kb[rows].astype(BF16)
        vat_ref[b] = va[rows].T.astype(BF16)
        vbt_ref[b] = vb[rows].T.astype(BF16)
        smb_ref[b] = sm[rows].astype(BF16)
        means.append(jnp.mean(ka[rows], axis=0, keepdims=True))
    km_ref[...] = jnp.concatenate(means, axis=0)


def _project(x, shift, scale, g_attn, w_main, w_small, bd, gq, gk, wi_scale):
    rows, d = x.shape
    nt = rows // ROW_TILE
    nb = rows // TK
    bpt = ROW_TILE // TK
    mrows = shift.shape[0]
    mod_map = (lambda i: (i, 0)) if mrows == rows else (lambda i: (0, 0))
    mod_blk = ROW_TILE if mrows == rows else 1
    row_spec = lambda w: pl.BlockSpec((ROW_TILE, w), lambda i: (i, 0))
    blk_spec = lambda w: pl.BlockSpec((bpt, TK, w), lambda i: (i, 0, 0))
    out_shape = (
        jax.ShapeDtypeStruct((rows, 1536), F32),
        jax.ShapeDtypeStruct((rows, 256), F32),
        jax.ShapeDtypeStruct((rows, 256), F32),
        jax.ShapeDtypeStruct((rows, 256), F32),
        jax.ShapeDtypeStruct((rows, 256), F32),
        jax.ShapeDtypeStruct((rows, 128), F32),
        jax.ShapeDtypeStruct((nb, TK, 256), BF16),
        jax.ShapeDtypeStruct((nb, TK, 256), BF16),
        jax.ShapeDtypeStruct((nb, 256, TK), BF16),
        jax.ShapeDtypeStruct((nb, 256, TK), BF16),
        jax.ShapeDtypeStruct((nb, TK, 128), BF16),
        jax.ShapeDtypeStruct((nt, bpt, 256), F32),
    )
    out_specs = (row_spec(1536), row_spec(256), row_spec(256), row_spec(256), row_spec(256),
                 row_spec(128), blk_spec(256), blk_spec(256), blk_spec(TK), blk_spec(TK),
                 blk_spec(128), pl.BlockSpec((None, bpt, 256), lambda i: (i, 0, 0)))
    in_specs = [row_spec(d),
                pl.BlockSpec((mod_blk, d), mod_map),
                pl.BlockSpec((mod_blk, d), mod_map),
                pl.BlockSpec((1, d), lambda i: (0, 0)),
                _resident(w_main.shape, lambda i: (0, 0)),
                _resident(w_small.shape, lambda i: (0, 0)),
                _resident(bd.shape, lambda i: (0, 0)),
                pl.BlockSpec(gq.shape, lambda i: (0, 0)),
                pl.BlockSpec(gk.shape, lambda i: (0, 0))]
    return pl.pallas_call(
        functools.partial(_proj_kernel, wi_scale=wi_scale),
        grid=(nt,), in_specs=in_specs, out_specs=out_specs, out_shape=out_shape,
        compiler_params=_cparams(1), name="adaln_in_projection",
    )(x, shift, scale, g_attn, w_main, w_small, bd, gq, gk)


def _store_padded_queries(qp_ref, q_ref):
    qT = q_ref[...].T
    zeros = jnp.zeros((64, 2 * TQ), BF16)
    for g in range(4):
        q2 = jnp.concatenate([qT[128 * g:128 * g + 64], qT[128 * g + 64:128 * g + 128]],
                             axis=1).astype(BF16)
        qp_ref[g] = jnp.concatenate([q2 if gg == g else zeros for gg in range(4)], axis=0)


def _for_tile_pairs(n, body, carry, per=2):
    start = 0
    while per >= 1:
        trips = (n - start) // per
        carry = lax.fori_loop(
            0, trips, lambda t, c, s=start, p=per: body([s + p * t + k for k in range(p)], c), carry)
        start = start + per * trips
        per //= 2
    return carry


def _attend(i, qp_ref, k_ref, vt_ref, bias_ref, o_ref, far_masks, diag_masks, stack):
    def tile_update(tiles, carry, tile_masks, near):
        kt = jnp.concatenate([k_ref[j] for j in tiles], axis=0)
        logits = [_dot(kt, qp_ref[g]) for g in range(4)]
        out = []
        for g in range(4):
            m, l, acc = carry[3 * g:3 * g + 3]
            mask = jnp.concatenate([jnp.broadcast_to(tm[g], (TK, 2 * TQ)) for tm in tile_masks], axis=0)
            s = logits[g] + mask
            if near is not None:
                s = s + jnp.concatenate(
                    [jnp.concatenate([bias_ref[2 * g, kind], bias_ref[2 * g + 1, kind]], axis=1)
                     for kind in near], axis=0)
            m_new = jnp.maximum(m, jnp.max(s, axis=0, keepdims=True))
            if g < 3:
                m_new = jnp.maximum(m_new, logits[g + 1][0:1, :] * 0.0 + NEG)
            alpha = jnp.exp2(m - m_new)
            p = jnp.exp2(s - m_new)
            l = alpha * l + jnp.sum(p, axis=0, keepdims=True)
            vt = jnp.concatenate([vt_ref[j, 64 * g:64 * g + 64, :] for j in tiles], axis=1)
            acc = alpha * acc + _dot(vt, p.astype(BF16))
            out.extend((m_new, l, acc))
        return tuple(out)

    carry = (jnp.full((1, 2 * TQ), NEG, F32), jnp.zeros((1, 2 * TQ), F32),
             jnp.zeros((HEAD_DIM, 2 * TQ), F32)) * 4
    n_far = jnp.maximum(i - 1, 0)
    carry = _for_tile_pairs(n_far, lambda tiles, c: tile_update(tiles, c, [far_masks(j) for j in tiles], None),
                            carry, per=stack)
    paired = jnp.minimum(i, 1)
    carry = lax.fori_loop(0, paired, lambda _, c: tile_update(
        [i - 1, i], c, [far_masks(i - 1), diag_masks()], [1, 0]), carry)
    carry = lax.fori_loop(paired, 1, lambda _, c: tile_update([i], c, [diag_masks()], [0]), carry)
    for g in range(4):
        out = carry[3 * g + 2] / carry[3 * g + 1]
        o_ref[128 * g:128 * g + 64, :] = out[:, :TQ]
        o_ref[128 * g + 64:128 * g + 128, :] = out[:, TQ:]


def _kth_largest(count_ge, counts_init, rmin, rmax, n_adm, n_sel):
    ksel = jnp.float32(n_sel)
    c_top, c_zero, c_pos = counts_init
    few = n_adm <= ksel
    top = ~few & (c_top >= ksel)
    zero = ~few & ~top & (c_pos < ksel) & (c_zero >= ksel)
    pos = c_pos >= ksel
    active = jnp.where(few | top | zero, 0.0, 1.0)
    tau = jnp.where(few, LOWEST, jnp.where(top, rmax, 0.0))
    check = jnp.where(top | zero, 1.0, 0.0)
    lo = jnp.where(pos, 0.0, rmin)
    hi = jnp.where(pos, rmax, 0.0)

    def cond(st):
        return (st[0] < MAX_BISECT) & (jnp.sum(st[1]) > 0.0)

    def step(st):
        it, active, lo, hi, tau, check = st
        mid = lo + (hi - lo) * 0.5
        stuck = (mid <= lo) | (mid >= hi)
        c = count_ge(mid)
        act = active > 0.0
        found = act & ~stuck & (c == ksel)
        ended = act & stuck
        ge = c >= ksel
        tau = jnp.where(found, mid, jnp.where(ended, lo, tau))
        check = jnp.where(ended, 1.0, check)
        lo = jnp.where(ge, mid, lo)
        hi = jnp.where(ge, hi, mid)
        active = jnp.where(found | ended, 0.0, active)
        return it + 1, active, lo, hi, tau, check

    _, active, lo, _, tau, check = lax.while_loop(
        cond, step, (jnp.int32(0), active, lo, hi, tau, check))
    left = active > 0.0
    return jnp.where(left, lo, tau), jnp.where(left, 1.0, check)


def _moba_kernel(q_ref, k_ref, vt_ref, km_ref, bias_ref, o_ref, qp_ref, mc_ref):
    i = pl.program_id(0)
    _store_padded_queries(qp_ref, q_ref)
    blk = lax.broadcasted_iota(jnp.int32, (128, 2 * TQ), 0).astype(F32)
    valid = blk < i.astype(F32)
    kmb = km_ref[...].astype(BF16)
    for g in range(4):
        gate = jnp.where(valid, _dot(kmb, qp_ref[g]), -jnp.inf)
        sel = jnp.zeros_like(gate)
        for _ in range(MOBA_TOPK):
            mx = jnp.max(gate, axis=0, keepdims=True)
            first = jnp.min(jnp.where(gate == mx, blk, 1e9), axis=0, keepdims=True)
            hit = blk == first
            sel = jnp.where(hit & valid, 1.0, sel)
            gate = jnp.where(hit, -jnp.inf, gate)
        mc_ref[g] = jnp.where(sel > 0.0, 0.0, NEG)
    key_io = lax.broadcasted_iota(jnp.int32, (TK, 2 * TQ), 0)
    qry_io = lax.broadcasted_iota(jnp.int32, (TK, 2 * TQ), 1) % TQ
    causal = jnp.where(key_io <= qry_io, 0.0, NEG)
    _attend(i, qp_ref, k_ref, vt_ref, bias_ref, o_ref,
            far_masks=lambda j: [mc_ref[g, pl.ds(j, 1), :] for g in range(4)],
            diag_masks=lambda: [causal] * 4, stack=MOBA_TILES)


def _moba_prompt(qf, kab, vat, kmean, bias):
    nb = kab.shape[0]
    t = nb * TK
    return pl.pallas_call(
        _moba_kernel,
        grid=(t // TQ,),
        in_specs=[pl.BlockSpec((TQ, 512), lambda i: (i, 0)),
                  _resident(kab.shape, lambda i: (0, 0, 0)),
                  _resident(vat.shape, lambda i: (0, 0, 0)),
                  _resident(kmean.shape, lambda i: (0, 0)),
                  _resident((8, 2, TK, TQ), lambda i: (0, 0, 0, 0))],
        out_specs=pl.BlockSpec((512, TQ), lambda i: (0, i)),
        out_shape=jax.ShapeDtypeStruct((512, t), F32),
        scratch_shapes=[pltpu.VMEM((4, 256, 2 * TQ), BF16), pltpu.VMEM((4, 128, 2 * TQ), F32)],
        compiler_params=_cparams(1), name="moba_prompt_attention",
    )(qf, kab, vat, kmean, bias)


def _fold8(x):
    return jnp.sum(x.reshape(TK // 8, 8, TQ), axis=0)


def _dsa_kernel(qb_ref, qi_ref, sm_ref, k_ref, vt_ref, ksm_ref, bias_ref, o_ref, qp_ref, sc_ref,
                *, n_sel):
    i = pl.program_id(0)
    qiT = qi_ref[...].T
    zeros = jnp.zeros((64, TQ), F32)
    qi_all = jnp.concatenate(
        [jnp.concatenate([qiT[64 * h:64 * h + 64], zeros], axis=0) for h in range(8)],
        axis=1).astype(BF16)
    wT = sm_ref[...].T
    w_rows = [wT[64 + h:65 + h, :] for h in range(8)]
    key_io = lax.broadcasted_iota(jnp.int32, (TK, TQ), 0)
    qry_io = lax.broadcasted_iota(jnp.int32, (TK, TQ), 1)
    causal = key_io <= qry_io

    def score_tile(j):
        z = _dot(ksm_ref[j], qi_all)
        sc = w_rows[0] * jnp.maximum(z[:, 0:TQ], 0.0)
        for h in range(1, 8):
            sc = sc + w_rows[h] * jnp.maximum(z[:, TQ * h:TQ * (h + 1)], 0.0)
        return sc

    def fold_max(x):
        return jnp.max(x.reshape(TK // 8, 8, TQ), axis=0)

    def stats(carry, hi, lo):
        mx, mn, c_zero, c_pos = carry
        return (jnp.maximum(mx, fold_max(hi)), jnp.minimum(mn, -fold_max(-lo)),
                c_zero + _fold8(jnp.where(hi >= 0.0, 1.0, 0.0)),
                c_pos + _fold8(jnp.where(hi > 0.0, 1.0, 0.0)))

    def fill(tiles, carry):
        for j in tiles:
            sc = score_tile(j)
            sc_ref[j] = sc
            carry = stats(carry, sc, sc)
        return carry

    carry = _for_tile_pairs(i, fill, (jnp.full((8, TQ), -jnp.inf, F32), jnp.full((8, TQ), jnp.inf, F32),
                                      jnp.zeros((8, TQ), F32), jnp.zeros((8, TQ), F32)), per=4)
    scd = score_tile(i)
    diag = jnp.where(causal, scd, -jnp.inf)
    sc_ref[i] = diag
    mx, mn, c_zero, c_pos = stats(carry, diag, jnp.where(causal, scd, jnp.inf))
    rmax = jnp.max(mx, axis=0, keepdims=True)
    rmin = jnp.min(mn, axis=0, keepdims=True)

    def count(pred):
        def body(tiles, acc):
            for j in tiles:
                acc = acc + _fold8(jnp.where(pred(sc_ref[j]), 1.0, 0.0))
            return acc
        return jnp.sum(_for_tile_pairs(i + 1, body, jnp.zeros((8, TQ), F32)), axis=0, keepdims=True)

    init = (count(lambda x: x >= rmax), jnp.sum(c_zero, axis=0, keepdims=True),
            jnp.sum(c_pos, axis=0, keepdims=True))
    n_adm = (i * TQ + 1).astype(F32) + lax.broadcasted_iota(jnp.int32, (1, TQ), 1).astype(F32)
    tau, check = _kth_largest(lambda t: count(lambda x: x >= t), init, rmin, rmax, n_adm, n_sel)

    @pl.when(jnp.sum(check) > 0.0)
    def _():
        need = jnp.float32(n_sel) - count(lambda x: x > tau)
        lower = jnp.where(lax.broadcasted_iota(jnp.int32, (TK, TK), 0)
                          >= lax.broadcasted_iota(jnp.int32, (TK, TK), 1), 1.0, 0.0).astype(BF16)

        def fix(j, seen):
            t = sc_ref[j]
            eq = jnp.where((t == tau) & (check > 0.0), 1.0, 0.0)
            rank = _dot(lower, eq.astype(BF16)) + seen
            sc_ref[j] = jnp.where((eq > 0.0) & (rank > need), -jnp.inf, t)
            return seen + jnp.sum(_fold8(eq), axis=0, keepdims=True)

        lax.fori_loop(0, i + 1, fix, jnp.zeros((1, TQ), F32))

    _store_padded_queries(qp_ref, qb_ref)

    def sel_masks(j):
        mb = jnp.where(sc_ref[j] >= tau, 0.0, NEG)
        return [jnp.concatenate([mb, mb], axis=1)] * 4

    _attend(i, qp_ref, k_ref, vt_ref, bias_ref, o_ref,
            far_masks=sel_masks, diag_masks=lambda: sel_masks(i), stack=DSA_TILES)


def _dsa_prompt(qf, sm, kbb, vbt, smb, bias, n_sel):
    nb = kbb.shape[0]
    t = nb * TK
    return pl.pallas_call(
        functools.partial(_dsa_kernel, n_sel=n_sel),
        grid=(t // TQ,),
        in_specs=[pl.BlockSpec((TQ, 512), lambda i: (i, 1)),
                  pl.BlockSpec((TQ, 512), lambda i: (i, 2)),
                  pl.BlockSpec((TQ, 128), lambda i: (i, 0)),
                  _resident(kbb.shape, lambda i: (0, 0, 0)),
                  _resident(vbt.shape, lambda i: (0, 0, 0)),
                  _resident(smb.shape, lambda i: (0, 0, 0)),
                  _resident((8, 2, TK, TQ), lambda i: (1, 0, 0, 0))],
        out_specs=pl.BlockSpec((512, TQ), lambda i: (0, i)),
        out_shape=jax.ShapeDtypeStruct((512, t), F32),
        scratch_shapes=[pltpu.VMEM((4, 256, 2 * TQ), BF16), pltpu.VMEM((nb, TK, TQ), F32)],
        compiler_params=_cparams(1), name="dsa_prompt_attention",
    )(qf, qf, sm, kbb, vbt, smb, bias)


def _ffn_kernel(x_ref, oa_ref, ob_ref, gt1_ref, sh2_ref, sc2_ref, gt2_ref, goa_ref, gob_ref, gf_ref,
                wo_ref, wg_ref, wu_ref, wd_ref, y_ref):
    def feat_norm(o, g):
        return o * lax.rsqrt(jnp.mean(o * o, axis=0, keepdims=True) + EPS) * g

    cat = jnp.concatenate([feat_norm(oa_ref[...], goa_ref[...]),
                           feat_norm(ob_ref[...], gob_ref[...])], axis=0)
    att = _dot(cat.T.astype(BF16), wo_ref[...])
    x1 = x_ref[...] + gt1_ref[...] * att
    xn = x1 * lax.rsqrt(jnp.mean(x1 * x1, axis=-1, keepdims=True) + EPS) * gf_ref[...]
    h = (xn * (1.0 + sc2_ref[...]) + sh2_ref[...]).astype(BF16)
    gate = _dot(h, wg_ref[...])
    up = _dot(h, wu_ref[...])
    act = (gate * jax.nn.sigmoid(gate) * up).astype(BF16)
    y_ref[...] = x1 + gt2_ref[...] * _dot(act, wd_ref[...])


def _out_ffn(x, oat, obt, gt1, sh2, sc2, gt2, goa, gob, g_ffn, wo, wg, wu, wd):
    rows, d = x.shape
    mrows = gt1.shape[0]
    mod_map = (lambda i: (i, 0)) if mrows == rows else (lambda i: (0, 0))
    mod_blk = FFN_TILE if mrows == rows else 1
    mod_spec = pl.BlockSpec((mod_blk, d), mod_map)
    col_spec = pl.BlockSpec((512, FFN_TILE), lambda i: (0, i))
    return pl.pallas_call(
        _ffn_kernel,
        grid=(rows // FFN_TILE,),
        in_specs=[pl.BlockSpec((FFN_TILE, d), lambda i: (i, 0)), col_spec, col_spec,
                  mod_spec, mod_spec, mod_spec, mod_spec,
                  pl.BlockSpec((512, 1), lambda i: (0, 0)),
                  pl.BlockSpec((512, 1), lambda i: (0, 0)),
                  pl.BlockSpec((1, d), lambda i: (0, 0)),
                  _resident(wo.shape, lambda i: (0, 0)),
                  _resident(wg.shape, lambda i: (0, 0)),
                  _resident(wu.shape, lambda i: (0, 0)),
                  _resident(wd.shape, lambda i: (0, 0))],
        out_specs=pl.BlockSpec((FFN_TILE, d), lambda i: (i, 0)),
        out_shape=jax.ShapeDtypeStruct((rows, d), F32),
        compiler_params=_cparams(1), name="out_projection_ffn",
    )(x, oat, obt, gt1, sh2, sc2, gt2, goa, gob, g_ffn, wo, wg, wu, wd)


def _stream_pages(pt_ref, hbm_refs, buf_refs, sem_ref, ppc, nc, chunk_body):
    n = pl.program_id(0)
    n_last = pl.num_programs(0) - 1

    def copy(a, page, slot, p):
        return pltpu.make_async_copy(hbm_refs[a].at[page], buf_refs[a].at[slot, p], sem_ref.at[slot, a])

    def start(batch, c, slot):
        for a in range(len(hbm_refs)):
            for p in range(ppc):
                copy(a, pt_ref[batch, c * ppc + p], slot, p).start()

    def wait(slot):
        for a in range(len(hbm_refs)):
            for p in range(ppc):
                copy(a, 0, slot, p).wait()

    @pl.when(n == 0)
    def _():
        start(0, 0, 0)

    for c in range(nc):
        slot = c % 2
        if c + 1 < nc:
            start(n, c + 1, 1 - slot)
        else:
            @pl.when(n < n_last)
            def _():
                start(n + 1, 0, 0)
        wait(slot)
        chunk_body(c, [buf.at[slot] for buf in buf_refs])


def _head_sum(z):
    out = z[0:8]
    for h in range(1, 8):
        out = out + z[8 * h:8 * h + 8]
    return out


def _sample_scan_kernel(pt_ref, mk_hbm, ik_hbm, qi_ref, w_ref, km_ref, sc_ref, mk_buf, ik_buf, sem,
                        *, ppc, nc):
    ppb = MOBA_BLOCK // 128
    bpc = ppc // ppb
    lane = lax.broadcasted_iota(jnp.int32, (256, 128), 1)
    qi = qi_ref[...]
    w = w_ref[...]
    km_ref[...] = jnp.zeros(km_ref.shape, F32)

    def chunk(c, bufs):
        mk, ik = bufs
        km = km_ref[...]
        for b in range(bpc):
            ksum = mk[ppb * b]
            for p in range(1, ppb):
                ksum = ksum + mk[ppb * b + p]
            kmean = jnp.sum(ksum, axis=1, keepdims=True) * (1.0 / MOBA_BLOCK)
            km = km + jnp.where(lane == c * bpc + b, kmean, 0.0)
        km_ref[...] = km
        for p in range(ppc):
            z = _dot(qi, ik[p].astype(BF16))
            col = (c * ppc + p) * 128
            sc_ref[:, col:col + 128] = _head_sum(jnp.maximum(z, 0.0) * w)

    _stream_pages(pt_ref, [mk_hbm, ik_hbm], [mk_buf, ik_buf], sem, ppc, nc, chunk)


def _sample_scan(page_table, cache_mk, cache_ik, qi_rows, w_col):
    n, n_pages = page_table.shape
    ppc = PAGES_PER_STEP
    nc = n_pages // ppc
    any_spec = pl.BlockSpec(memory_space=pl.ANY)
    grid_spec = pltpu.PrefetchScalarGridSpec(
        num_scalar_prefetch=1, grid=(n,),
        in_specs=[any_spec, any_spec,
                  pl.BlockSpec((None, 64, 64), lambda b, pt: (b, 0, 0)),
                  pl.BlockSpec((None, 64, 1), lambda b, pt: (b, 0, 0))],
        out_specs=(pl.BlockSpec((None, 256, 128), lambda b, pt: (b, 0, 0)),
                   pl.BlockSpec((None, 8, n_pages * 128), lambda b, pt: (b, 0, 0))),
        scratch_shapes=[pltpu.VMEM((2, ppc, 256, 128), F32), pltpu.VMEM((2, ppc, 64, 128), F32),
                        pltpu.SemaphoreType.DMA((2, 2))])
    return pl.pallas_call(
        functools.partial(_sample_scan_kernel, ppc=ppc, nc=nc),
        grid_spec=grid_spec,
        out_shape=(jax.ShapeDtypeStruct((n, 256, 128), F32),
                   jax.ShapeDtypeStruct((n, 8, n_pages * 128), F32)),
        compiler_params=_cparams(1), name="sample_page_scan",
    )(page_table, cache_mk, cache_ik, qi_rows, w_col)


def _sample_select_kernel(sc_ref, km_ref, qa_ref, qi_ref, w_ref, kin_ref, bm_ref, mb_ref,
                          *, n_blocks, n_new, n_sel, idx_iter):
    nb = SELECT_BATCH
    rows = nb * 64
    lane = lax.broadcasted_iota(jnp.int32, (rows, 128), 1).astype(F32)
    valid = lane < float(n_blocks)
    gate = jnp.concatenate([_dot(qa_ref[b], km_ref[b].astype(BF16)) for b in range(nb)], axis=0)
    gate = jnp.where(valid, gate, -jnp.inf)
    sel = jnp.zeros_like(gate)
    for _ in range(min(MOBA_TOPK, n_blocks)):
        mx = jnp.max(gate, axis=1, keepdims=True)
        first = jnp.min(jnp.where(gate == mx, lane, 1e9), axis=1, keepdims=True)
        hit = lane == first
        sel = jnp.where(hit & valid, 1.0, sel)
        gate = jnp.where(hit, -jnp.inf, gate)
    bm_ref[...] = jnp.where(sel > 0.0, 0.0, NEG).reshape(nb, 64, 128)

    new = jnp.concatenate(
        [_head_sum(jnp.maximum(_dot(qi_ref[b], kin_ref[b]), 0.0) * w_ref[b]) for b in range(nb)], axis=0)
    tok = lax.broadcasted_iota(jnp.int32, (nb * 8, 128), 0) % 8
    col = lax.broadcasted_iota(jnp.int32, (nb * 8, 128), 1)
    new = jnp.where((col <= tok) & (col < n_new), new, -jnp.inf)
    past = sc_ref.shape[-1]
    a = jnp.concatenate([sc_ref[...].reshape(nb * 8, past), new], axis=1)
    width = past + 128
    idx = lax.broadcasted_iota(jnp.int32, (nb * 8, width), 1).astype(F32)
    adm = a > -jnp.inf

    def count(mask):
        return jnp.sum(jnp.where(mask, 1.0, 0.0), axis=1, keepdims=True)

    rmax = jnp.max(a, axis=1, keepdims=True)
    rmin = jnp.min(jnp.where(adm, a, jnp.inf), axis=1, keepdims=True)
    init = (count(a >= rmax), count(a >= 0.0), count(a > 0.0))
    tau, check = _kth_largest(lambda t: count(a >= t), init, rmin, rmax, count(adm), n_sel)

    eq = a == tau
    need = jnp.float32(n_sel) - count(a > tau)

    def bis(_, carry):
        lj, hj = carry
        mj = jnp.floor((lj + hj) * 0.5)
        ok = count(eq & (idx <= mj)) >= need
        return jnp.where(ok, lj, mj), jnp.where(ok, mj, hj)

    n_bis = jnp.where(jnp.sum(check) > 0.0, idx_iter, 0)
    _, hj = lax.fori_loop(0, n_bis, bis, (jnp.full((nb * 8, 1), -1.0, F32),
                                          jnp.full((nb * 8, 1), float(width - 1), F32)))
    hj = jnp.where(check > 0.0, hj, float(width))
    keep = (a > tau) | (eq & (idx <= hj))
    mb_ref[...] = jnp.where(keep, 0.0, NEG).reshape(nb, 8, width)


def _sample_select(scores, kmean_t, qa_rows, qi_rows, w_col, ki_new_t, n_blocks, n_new, n_sel):
    n, _, past = scores.shape
    nb = SELECT_BATCH
    width = past + 128
    kern = functools.partial(_sample_select_kernel, n_blocks=n_blocks, n_new=n_new, n_sel=n_sel,
                             idx_iter=int(math.ceil(math.log2(width))) + 1)
    per_n = lambda *shape: pl.BlockSpec((nb,) + shape, lambda b: (b,) + (0,) * len(shape))
    return pl.pallas_call(
        kern, grid=(n // nb,),
        in_specs=[per_n(8, past), per_n(256, 128), per_n(64, 256), per_n(64, 64), per_n(64, 1),
                  per_n(64, 128)],
        out_specs=(per_n(64, 128), per_n(8, width)),
        out_shape=(jax.ShapeDtypeStruct((n, 64, 128), F32),
                   jax.ShapeDtypeStruct((n, 8, width), F32)),
        compiler_params=_cparams(1), name="sample_selection",
    )(scores, kmean_t, qa_rows, qi_rows, w_col, ki_new_t)


def _row_update(s, pv, state):
    m, l, acc = state
    m_new = jnp.maximum(m, jnp.max(s, axis=1, keepdims=True))
    alpha = jnp.exp2(m - m_new)
    p = jnp.exp2(s - m_new)
    return (m_new, alpha * l + jnp.sum(p, axis=1, keepdims=True), alpha * acc + pv(p.astype(BF16)))


def _sample_attn_kernel(pt_ref, mk_hbm, mv_hbm, dk_hbm, dv_hbm, qa_ref, qb_ref, bm_ref, mb_ref, bl_ref,
                        bn_ref, kan_ref, van_ref, kbn_ref, vbn_ref, oa_ref, ob_ref,
                        mk_buf, mv_buf, dk_buf, dv_buf, sem, *, ppc, nc, n_new):
    ppb = MOBA_BLOCK // 128
    bpc = ppc // ppb
    lane = lax.broadcasted_iota(jnp.int32, (64, 128), 1)
    tile8 = lambda x: jnp.concatenate([x] * 8, axis=0)
    qa = qa_ref[...]
    qb = qb_ref[...]
    bm = bm_ref[...]
    fresh = (jnp.full((64, 1), NEG, F32), jnp.zeros((64, 1), F32), jnp.zeros((64, 256), F32))
    states = [fresh, fresh]

    def logits(q, pages):
        return jnp.concatenate([_dot(q, pages[p].astype(BF16)) for p in range(ppc)], axis=1)

    def values(pages):
        def pv(p):
            out = _dot_nt(p[:, 0:128], pages[0].astype(BF16))
            for k in range(1, ppc):
                out = out + _dot_nt(p[:, 128 * k:128 * (k + 1)], pages[k].astype(BF16))
            return out
        return pv

    def chunk(c, bufs):
        mk, mv, dk, dv = bufs
        gate = jnp.concatenate(
            [jnp.broadcast_to(jnp.sum(jnp.where(lane == c * bpc + b, bm, 0.0), axis=1, keepdims=True),
                              (64, MOBA_BLOCK)) for b in range(bpc)], axis=1)
        sa = logits(qa, mk) + gate
        sb = logits(qb, dk) + tile8(mb_ref[:, c * ppc * 128:(c + 1) * ppc * 128])
        if c == nc - 1:
            pad = jnp.zeros((64, (ppc - 1) * 128), F32)
            sa = sa + jnp.concatenate([pad, bl_ref[0]], axis=1)
            sb = sb + jnp.concatenate([pad, bl_ref[1]], axis=1)
        states[0] = _row_update(sa, values(mv), states[0])
        states[1] = _row_update(sb, values(dv), states[1])

    _stream_pages(pt_ref, [mk_hbm, mv_hbm, dk_hbm, dv_hbm], [mk_buf, mv_buf, dk_buf, dv_buf], sem,
                  ppc, nc, chunk)

    tok = lax.broadcasted_iota(jnp.int32, (64, 128), 0) % 8
    own = (lane <= tok) & (lane < n_new)
    s = jnp.where(own, _dot_nt(qa, kan_ref[...]) + bn_ref[0], NEG)
    states[0] = _row_update(s, lambda p: _dot(p, van_ref[...]), states[0])
    s = _dot_nt(qb, kbn_ref[...]) + bn_ref[1] + tile8(mb_ref[:, nc * ppc * 128:nc * ppc * 128 + 128])
    states[1] = _row_update(s, lambda p: _dot(p, vbn_ref[...]), states[1])
    grp = lax.broadcasted_iota(jnp.int32, (64, 64), 0) // 16
    for o_ref, (m, l, acc) in ((oa_ref, states[0]), (ob_ref, states[1])):
        o = acc / l
        out = jnp.zeros((64, 64), F32)
        for g in range(4):
            out = out + jnp.where(grp == g, o[:, 64 * g:64 * g + 64], 0.0)
        o_ref[...] = out


def _sample_attend(page_table, cache_mk, cache_mv, cache_dk, cache_dv, qa_rows, qb_rows, bm, mb,
                   bias_last, bias_new, ka_new, va_new, kb_new, vb_new, n_new):
    n, n_pages = page_table.shape
    ppc = PAGES_PER_STEP
    nc = n_pages // ppc
    width = mb.shape[-1]
    any_spec = pl.BlockSpec(memory_space=pl.ANY)
    per_n = lambda *shape: pl.BlockSpec((None,) + shape, lambda b, pt: (b,) + (0,) * len(shape))
    const = lambda *shape: pl.BlockSpec(shape, lambda b, pt: (0,) * len(shape))
    in_specs = [any_spec] * 4 + [
        per_n(64, 256), per_n(64, 256), per_n(64, 128), per_n(8, width),
        const(2, 64, 128), const(2, 64, 128),
        per_n(128, 256), per_n(128, 256), per_n(128, 256), per_n(128, 256)]
    grid_spec = pltpu.PrefetchScalarGridSpec(
        num_scalar_prefetch=1, grid=(n,), in_specs=in_specs,
        out_specs=(per_n(64, 64), per_n(64, 64)),
        scratch_shapes=[pltpu.VMEM((2, ppc, 256, 128), F32)] * 4 + [pltpu.SemaphoreType.DMA((2, 4))])
    return pl.pallas_call(
        functools.partial(_sample_attn_kernel, ppc=ppc, nc=nc, n_new=n_new),
        grid_spec=grid_spec,
        out_shape=(jax.ShapeDtypeStruct((n, 64, 64), F32), jax.ShapeDtypeStruct((n, 64, 64), F32)),
        compiler_params=_cparams(1), name="sample_paged_attention",
    )(page_table, cache_mk, cache_mv, cache_dk, cache_dv,
      qa_rows, qb_rows, bm, mb, bias_last, bias_new, ka_new, va_new, kb_new, vb_new)


def _head_token_rows(q, n, s):
    q = q.reshape(n, s, 8, HEAD_DIM).transpose(0, 2, 1, 3)
    q = jnp.pad(q, ((0, 0), (0, 0), (0, 8 - s), (0, 0)))
    return q.reshape(n, 64, HEAD_DIM)


def _block_diag_rows(q_rows):
    n = q_rows.shape[0]
    grp = jnp.arange(64) // 16
    onehot = (grp[:, None] == jnp.arange(4)[None, :]).astype(q_rows.dtype)
    return (q_rows[:, :, None, :] * onehot[None, :, :, None]).reshape(n, 64, 256)


def _pad_new_rows(x, n, s):
    w = x.shape[1]
    return jnp.pad(x.reshape(n, s, w), ((0, 0), (0, 128 - s), (0, 0))).astype(BF16)


def _features_major_pages(cache):
    n_pool, page = cache.shape[:2]
    return jnp.moveaxis(cache.reshape(n_pool, page, -1), 1, 2)


def kernel(x_prompt, x_sample, cache_moba_k, cache_moba_v, cache_dsa_k, cache_dsa_v, cache_idx_k,
           page_table, c_prompt, c_sample, rel_bias, g_attn, g_ffn, w_ada, b_ada, w_in, g_qa, g_ka,
           g_qb, g_kb, g_oa, g_ob, w_out, w_gate, w_up, w_down):
    batch, seq, d = x_prompt.shape
    n_dec, s_dec, _ = x_sample.shape
    depth = w_in.shape[0]
    page = cache_idx_k.shape[2]
    n_pages = page_table.shape[1]
    past = n_pages * page
    assert batch == 1 and d == 1024 and rel_bias.shape == (N_BUCKETS, 16)
    assert seq % ROW_TILE == 0 and (n_dec * s_dec) % ROW_TILE == 0 and s_dec <= 8
    assert page == 128 and past % MOBA_BLOCK == 0 and n_pages % (2 * PAGES_PER_STEP) == 0
    assert seq // MOBA_BLOCK <= 128 and past // MOBA_BLOCK <= 128 and n_dec % SELECT_BATCH == 0
    rows_s = n_dec * s_dec
    n_sel_p = min(DSA_TOPK, seq // 4)
    n_sel_s = min(DSA_TOPK, (past + s_dec) // 4)
    idx_width = 512
    wi_scale = float(idx_width) ** -0.5

    bias_p = _bias_tiles(rel_bias, _prompt_bias_patterns())
    bias_s = _bias_tiles(rel_bias, _sample_bias_patterns(s_dec, page))
    bias_last = bias_s[:, 0].reshape(2, 64, 128)
    bias_new = bias_s[:, 1].reshape(2, 64, 128)
    bd = jnp.asarray(np.kron(np.eye(8, dtype=np.float32), np.ones((64, 64), np.float32)), BF16)

    xp = x_prompt.reshape(seq, d)
    xs = x_sample.reshape(rows_s, d)
    pad_rows = (-(1 + n_dec)) % 8
    c_all = jnp.concatenate([c_prompt, c_sample, jnp.zeros((pad_rows, d), F32)], axis=0)
    outs_p, outs_s = [], []
    for l in range(depth):
        mod = _modulation(c_all, w_ada[l], b_ada[l][None, :])
        mod_p = [mod[0:1, k * d:(k + 1) * d] for k in range(6)]
        mod_s = [jnp.repeat(mod[1:1 + n_dec, k * d:(k + 1) * d], s_dec, axis=0) for k in range(6)]
        w_main = w_in[l][:, 0:2560].astype(BF16)
        w_small = jnp.pad(w_in[l][:, 2560:2632], ((0, 0), (0, 56))).astype(BF16)
        gq = jnp.stack([jnp.tile(g_qa[l], 8), jnp.tile(g_qb[l], 8)])
        gk = jnp.stack([jnp.tile(g_ka[l], 4), jnp.tile(g_kb[l], 4)])
        goa, gob = g_oa[l][:, None], g_ob[l][:, None]
        wo, wg, wu, wd = (w_out[l].astype(BF16), w_gate[l].astype(BF16), w_up[l].astype(BF16),
                          w_down[l].astype(BF16))

        (qf, ka, va, kb, vb, sm, kab, kbb, vat, vbt, smb, km) = _project(
            xp, mod_p[0], mod_p[1], g_attn[l][None, :], w_main, w_small, bd, gq, gk, wi_scale)
        nb = seq // MOBA_BLOCK
        kmean = jnp.pad(km.reshape(nb, 256), ((0, 128 - nb), (0, 0)))
        oat = _moba_prompt(qf, kab, vat, kmean, bias_p)
        obt = _dsa_prompt(qf, sm, kbb, vbt, smb, bias_p, n_sel_p)
        xp = _out_ffn(xp, oat, obt, mod_p[2], mod_p[3], mod_p[4], mod_p[5], goa, gob,
                      g_ffn[l][None, :], wo, wg, wu, wd)
        outs_p.append((ka.reshape(1, seq, 4, 64), va.reshape(1, seq, 4, 64), kb.reshape(1, seq, 4, 64),
                       vb.reshape(1, seq, 4, 64), sm[:, :64].reshape(1, seq, 64)))

        (qf, ka, va, kb, vb, sm, _, _, _, _, _, _) = _project(
            xs, mod_s[0], mod_s[1], g_attn[l][None, :], w_main, w_small, bd, gq, gk, wi_scale)
        qa_rows = _block_diag_rows(_head_token_rows(qf[:, 0:512], n_dec, s_dec)).astype(BF16)
        qb_rows = _block_diag_rows(_head_token_rows(qf[:, 512:1024], n_dec, s_dec)).astype(BF16)
        qi_rows = _head_token_rows(qf[:, 1024:1536], n_dec, s_dec).astype(BF16)
        w_col = jnp.pad(sm[:, 64:72].reshape(n_dec, s_dec, 8).transpose(0, 2, 1),
                        ((0, 0), (0, 0), (0, 8 - s_dec))).reshape(n_dec, 64, 1)
        cmk = _features_major_pages(cache_moba_k[l])
        cmv = _features_major_pages(cache_moba_v[l])
        cdk = _features_major_pages(cache_dsa_k[l])
        cdv = _features_major_pages(cache_dsa_v[l])
        cik = _features_major_pages(cache_idx_k[l])
        km_s, scores = _sample_scan(page_table, cmk, cik, qi_rows, w_col)
        nbp = past // MOBA_BLOCK
        ki_new_t = jnp.swapaxes(_pad_new_rows(sm[:, :64], n_dec, s_dec), 1, 2)
        bm, mb = _sample_select(scores, km_s, qa_rows, qi_rows, w_col, ki_new_t, nbp, s_dec, n_sel_s)
        oa_s, ob_s = _sample_attend(
            page_table, cmk, cmv, cdk, cdv, qa_rows, qb_rows, bm, mb, bias_last, bias_new,
            _pad_new_rows(ka, n_dec, s_dec), _pad_new_rows(va, n_dec, s_dec),
            _pad_new_rows(kb, n_dec, s_dec), _pad_new_rows(vb, n_dec, s_dec), s_dec)
        to_t = lambda o: o.reshape(n_dec, 8, 8, 64)[:, :, :s_dec].transpose(1, 3, 0, 2).reshape(512, rows_s)
        xs = _out_ffn(xs, to_t(oa_s), to_t(ob_s), mod_s[2], mod_s[3], mod_s[4], mod_s[5], goa, gob,
                      g_ffn[l][None, :], wo, wg, wu, wd)
        outs_s.append((ka.reshape(n_dec, s_dec, 4, 64), va.reshape(n_dec, s_dec, 4, 64),
                       kb.reshape(n_dec, s_dec, 4, 64), vb.reshape(n_dec, s_dec, 4, 64),
                       sm[:, :64].reshape(n_dec, s_dec, 64)))

    stack = lambda outs, k: jnp.stack([o[k] for o in outs])
    return (xp.reshape(batch, seq, d), xs.reshape(n_dec, s_dec, d),
            *[stack(outs_p, k) for k in range(5)], *[stack(outs_s, k) for k in range(5)])
```
